```python
import jax, jax.numpy as jnp
from jax import lax
import numpy as np

D_MODEL = 2048
BATCH = 16
SEQ = 256
DEPTH = 2
DEC_BATCH = 4
DEC_SEQ = 1024
PAST_LEN = 512

GRID_W = 64
N_MIXERS = 2
N_ATTN_LAYERS = (DEPTH + 1) // 2
N_LRU_LAYERS = DEPTH // 2
HEAD_DIM = 128
N_HEADS = D_MODEL // HEAD_DIM
N_KV_HEADS = N_HEADS // 4
Q_PER_KV = N_HEADS // N_KV_HEADS
ATTN_WIDTH = N_HEADS * HEAD_DIM
KV_WIDTH = N_KV_HEADS * HEAD_DIM
AXIS_DIM = HEAD_DIM // 2
ROPE_THETA = 10000.0
Q_BLOCK = 128
LRU_WIDTH = D_MODEL
LRU_BLOCKS = 8
LRU_BLOCK_DIM = LRU_WIDTH // LRU_BLOCKS
CONV_WIDTH = 4
CONV_LEFT = (CONV_WIDTH - 1) // 2
CONV_RIGHT = CONV_WIDTH - 1 - CONV_LEFT
RG_C = 8.0
EPS = 1e-6

kernel_name = 'hybrid_dit_gqa_rglru_step'

F32 = jnp.float32


def rms_norm(x, g):
    xf = x.astype(F32)
    y = xf * lax.rsqrt(jnp.mean(xf * xf, axis=-1, keepdims=True) + EPS)
    return (y * g.astype(F32)).astype(x.dtype)


def adaln(cond, w_mod, b_mod):
    m = jax.nn.silu(cond) @ w_mod + b_mod
    shift, scale, gate = jnp.split(m[:, None, :], 3, axis=-1)
    return shift, scale, gate


def axial_rope_tables(n):
    rows = n // GRID_W
    row = jnp.repeat(jnp.arange(rows, dtype=F32), GRID_W)
    col = jnp.tile(jnp.arange(GRID_W, dtype=F32), rows)
    inv = ROPE_THETA ** (-jnp.arange(0, AXIS_DIM, 2, dtype=F32) / AXIS_DIM)
    ar = row[:, None] * inv
    ac = col[:, None] * inv
    ang = jnp.concatenate([ar, ar, ac, ac], axis=-1)
    return jnp.cos(ang), jnp.sin(ang)


def rotate_half(u):
    h = u.shape[-1] // 2
    return jnp.concatenate([-u[..., h:], u[..., :h]], axis=-1)


def apply_axial_rope(x, cos, sin):
    rot = jnp.concatenate([rotate_half(x[..., :AXIS_DIM]), rotate_half(x[..., AXIS_DIM:])], axis=-1)
    out = x.astype(F32) * cos[None, :, None, :] + rot.astype(F32) * sin[None, :, None, :]
    return out.astype(x.dtype)


def blocked_attention(q, k, v):
    b, nq = q.shape[0], q.shape[1]
    nb = nq // Q_BLOCK
    qb = q.reshape(b, nb, Q_BLOCK, N_KV_HEADS, Q_PER_KV, HEAD_DIM).swapaxes(0, 1)
    scale = HEAD_DIM ** -0.5

    def one_block(qblk):
        s = jnp.einsum('bqkgd,bskd->bkgqs', qblk, k, preferred_element_type=F32) * scale
        p = jax.nn.softmax(s, axis=-1).astype(v.dtype)
        return jnp.einsum('bkgqs,bskd->bqkgd', p, v)

    o = lax.map(one_block, qb)
    return o.swapaxes(0, 1).reshape(b, nq, ATTN_WIDTH)


def attn_in(h, w_in, q_norm, k_norm):
    b, n, _ = h.shape
    p = h @ w_in
    q, k, v, g = jnp.split(p, [ATTN_WIDTH, ATTN_WIDTH + KV_WIDTH, ATTN_WIDTH + 2 * KV_WIDTH], axis=-1)
    q = rms_norm(q.reshape(b, n, N_HEADS, HEAD_DIM), q_norm)
    k = rms_norm(k.reshape(b, n, N_KV_HEADS, HEAD_DIM), k_norm)
    v = v.reshape(b, n, N_KV_HEADS, HEAD_DIM)
    return q, k, v, g


def attn_context(h, w_in, q_norm, k_norm, w_out):
    q, k, v, g = attn_in(h, w_in, q_norm, k_norm)
    o = blocked_attention(q, k, v)
    return (o * jax.nn.silu(g)) @ w_out, k, v


def attn_latent(h, ck, cv, w_in, q_norm, k_norm, w_out):
    q, k, v, g = attn_in(h, w_in, q_norm, k_norm)
    cos, sin = axial_rope_tables(h.shape[1])
    q = apply_axial_rope(q, cos, sin)
    k = apply_axial_rope(k, cos, sin)
    keys = jnp.concatenate([ck.astype(k.dtype), k], axis=1)
    vals = jnp.concatenate([cv.astype(v.dtype), v], axis=1)
    o = blocked_attention(q, keys, vals)
    return (o * jax.nn.silu(g)) @ w_out


def centred_depthwise_conv(x, w, b):
    n = x.shape[1]
    xp = jnp.pad(x, ((0, 0), (CONV_LEFT, CONV_RIGHT), (0, 0)))
    return sum((xp[:, j:j + n] * w[j] for j in range(CONV_WIDTH)), b)


def block_diag_linear(x, w, b):
    bsz, n, _ = x.shape
    y = jnp.einsum('bnhi,hij->bnhj', x.reshape(bsz, n, LRU_BLOCKS, LRU_BLOCK_DIM), w)
    return y.reshape(bsz, n, LRU_WIDTH) + b


def rglru_coeffs(x, w_a, b_a, w_x, b_x, lam):
    r = jax.nn.sigmoid(block_diag_linear(x, w_a, b_a).astype(F32))
    i = jax.nn.sigmoid(block_diag_linear(x, w_x, b_x).astype(F32))
    log_a = RG_C * r * jax.nn.log_sigmoid(lam.astype(F32))
    a = jnp.exp(log_a)
    mult = jnp.sqrt(-jnp.expm1(2.0 * log_a))
    return a, mult * i * x.astype(F32)


def _combine(e1, e2):
    a1, b1 = e1
    a2, b2 = e2
    return a1 * a2, a2 * b1 + b2


def linear_scan(a, b, h0, reverse):
    idx = -1 if reverse else 0
    b = b.at[:, idx].add(a[:, idx] * h0)
    _, h = lax.associative_scan(_combine, (a, b), axis=1, reverse=reverse)
    return h


def lru_mixer(h, h0_fwd, h0_bwd, w_in, conv_w, conv_b, w_a, b_a, w_x, b_x, lam, w_out):
    xb, gb = jnp.split(h @ w_in, 2, axis=-1)
    xb = centred_depthwise_conv(xb, conv_w, conv_b)
    hs = []
    finals = []
    for d, (h0, rev) in enumerate(((h0_fwd, False), (h0_bwd, True))):
        a, b = rglru_coeffs(xb, w_a[d], b_a[d], w_x[d], b_x[d], lam[d])
        hd = linear_scan(a, b, h0.astype(F32), rev)
        hs.append(hd)
        finals.append(hd[:, 0] if rev else hd[:, -1])
    y = (hs[0] + hs[1]).astype(h.dtype)
    out = (y * jax.nn.silu(gb)) @ w_out
    return out, jnp.stack(finals, axis=1).astype(h.dtype)


def setup_inputs(seed: int = 0) -> dict:
    key = jax.random.key(seed)
    ks = jax.random.split(key, 24)

    def nrm(k, shape, std):
        return std * jax.random.normal(k, shape, F32)

    a0 = jax.random.uniform(ks[22], (N_LRU_LAYERS, 2, LRU_WIDTH), F32, minval=0.9, maxval=0.999)
    s = a0 ** (1.0 / RG_C)
    lru_lambda = jnp.log(s) - jnp.log1p(-s)
    in_w = ATTN_WIDTH + 2 * KV_WIDTH + ATTN_WIDTH
    return {
        'x_prompt': nrm(ks[0], (BATCH, SEQ, D_MODEL), 1.0),
        'x_sample': nrm(ks[1], (DEC_BATCH, DEC_SEQ, D_MODEL), 1.0),
        'c': nrm(ks[2], (DEC_BATCH, D_MODEL), 1.0),
        'cache_k': nrm(ks[3], (DEC_BATCH, N_ATTN_LAYERS, PAST_LEN, N_KV_HEADS, HEAD_DIM), 1.0),
        'cache_v': nrm(ks[4], (DEC_BATCH, N_ATTN_LAYERS, PAST_LEN, N_KV_HEADS, HEAD_DIM), 1.0),
        'state_lru': nrm(ks[5], (DEC_BATCH, N_LRU_LAYERS, 2, LRU_WIDTH), 0.5),
        'c_ctx': nrm(ks[6], (D_MODEL,), 1.0),
        'w_mod': nrm(ks[7], (DEPTH, D_MODEL, 3 * D_MODEL), 0.5 * D_MODEL ** -0.5),
        'b_mod': nrm(ks[8], (DEPTH, 3 * D_MODEL), 0.01),
        'g_pre': 1.0 + nrm(ks[9], (DEPTH, D_MODEL), 0.05),
        'g_post': 1.0 + nrm(ks[10], (DEPTH, D_MODEL), 0.05),
        'w_in_attn': nrm(ks[11], (N_ATTN_LAYERS, D_MODEL, in_w), D_MODEL ** -0.5),
        'q_norm': 1.0 + nrm(ks[12], (N_ATTN_LAYERS, HEAD_DIM), 0.05),
        'k_norm': 1.0 + nrm(ks[13], (N_ATTN_LAYERS, HEAD_DIM), 0.05),
        'w_out_attn': nrm(ks[14], (N_ATTN_LAYERS, ATTN_WIDTH, D_MODEL), ATTN_WIDTH ** -0.5),
        'w_in_lru': nrm(ks[15], (N_LRU_LAYERS, D_MODEL, 2 * LRU_WIDTH), D_MODEL ** -0.5),
        'conv_w': nrm(ks[16], (N_LRU_LAYERS, CONV_WIDTH, LRU_WIDTH), CONV_WIDTH ** -0.5),
        'conv_b': nrm(ks[17], (N_LRU_LAYERS, LRU_WIDTH), 0.01),
        'w_rg_a': nrm(ks[18], (N_LRU_LAYERS, 2, LRU_BLOCKS, LRU_BLOCK_DIM, LRU_BLOCK_DIM), LRU_BLOCK_DIM ** -0.5),
        'b_rg_a': nrm(ks[19], (N_LRU_LAYERS, 2, LRU_WIDTH), 0.01),
        'w_rg_x': nrm(ks[20], (N_LRU_LAYERS, 2, LRU_BLOCKS, LRU_BLOCK_DIM, LRU_BLOCK_DIM), LRU_BLOCK_DIM ** -0.5),
        'b_rg_x': nrm(ks[21], (N_LRU_LAYERS, 2, LRU_WIDTH), 0.01),
        'lru_lambda': lru_lambda,
        'w_out_lru': nrm(ks[23], (N_LRU_LAYERS, LRU_WIDTH, D_MODEL), LRU_WIDTH ** -0.5),
    }


def reference(x_prompt, x_sample, c, cache_k, cache_v, state_lru, c_ctx, w_mod, b_mod, g_pre, g_post,
              w_in_attn, q_norm, k_norm, w_out_attn, w_in_lru, conv_w, conv_b,
              w_rg_a, b_rg_a, w_rg_x, b_rg_x, lru_lambda, w_out_lru):
    y_p = x_prompt
    y_s = x_sample
    new_k, new_v, new_h = [], [], []
    for l in range(DEPTH):
        j = l // N_MIXERS
        sh_c, sc_c, ga_c = adaln(c_ctx[None, :], w_mod[l], b_mod[l])
        sh_s, sc_s, ga_s = adaln(c, w_mod[l], b_mod[l])
        hp = rms_norm(y_p, g_pre[l]) * (1.0 + sc_c) + sh_c
        hs = rms_norm(y_s, g_pre[l]) * (1.0 + sc_s) + sh_s
        if l % N_MIXERS == 0:
            mp, kc, vc = attn_context(hp, w_in_attn[j], q_norm[j], k_norm[j], w_out_attn[j])
            ms = attn_latent(hs, cache_k[:, j], cache_v[:, j], w_in_attn[j], q_norm[j], k_norm[j], w_out_attn[j])
            new_k.append(kc)
            new_v.append(vc)
        else:
            zeros = jnp.zeros((hp.shape[0], LRU_WIDTH), F32)
            mp, fin = lru_mixer(hp, zeros, zeros, w_in_lru[j], conv_w[j], conv_b[j],
                                w_rg_a[j], b_rg_a[j], w_rg_x[j], b_rg_x[j], lru_lambda[j], w_out_lru[j])
            ms, _ = lru_mixer(hs, state_lru[:, j, 0], state_lru[:, j, 1], w_in_lru[j], conv_w[j], conv_b[j],
                              w_rg_a[j], b_rg_a[j], w_rg_x[j], b_rg_x[j], lru_lambda[j], w_out_lru[j])
            new_h.append(fin)
        y_p = y_p + ga_c * rms_norm(mp, g_post[l])
        y_s = y_s + ga_s * rms_norm(ms, g_post[l])
    new_cache_k = jnp.stack(new_k, axis=1)
    new_cache_v = jnp.stack(new_v, axis=1)
    new_state_lru = jnp.stack(new_h, axis=1)
    return (y_p, y_s, new_cache_k, new_cache_v, new_state_lru)
```

```python
import functools

import jax
import jax.numpy as jnp
from jax import lax
from jax.experimental import pallas as pl
from jax.experimental.pallas import tpu as pltpu

F32 = jnp.float32
BF16 = jnp.bfloat16

D_MODEL = 2048
HEAD_DIM = 128
N_HEADS = 16
N_KV_HEADS = 4
Q_PER_KV = N_HEADS // N_KV_HEADS
ATTN_WIDTH = N_HEADS * HEAD_DIM
KV_WIDTH = N_KV_HEADS * HEAD_DIM
GRID_W = 64
AXIS_DIM = HEAD_DIM // 2
ROPE_THETA = 10000.0
LRU_WIDTH = D_MODEL
LRU_BLOCKS = 8
LRU_BLOCK_DIM = LRU_WIDTH // LRU_BLOCKS
CONV_WIDTH = 4
CONV_LEFT = (CONV_WIDTH - 1) // 2
CONV_RIGHT = CONV_WIDTH - 1 - CONV_LEFT
RG_C = 8.0
EPS = 1e-6

LANES = 128
N_SLABS = D_MODEL // LANES
COND_ROWS = 8
VMEM_LIMIT = 56 * 1024 * 1024


def _params(sem):
    return pltpu.CompilerParams(dimension_semantics=sem, vmem_limit_bytes=VMEM_LIMIT)


def _row_loop(n_rows, chunk, body):
    def step(i, carry):
        body(pl.multiple_of(i * chunk, chunk))
        return carry
    lax.fori_loop(0, n_rows // chunk, step, 0)


def _rms(x, width):
    return x * lax.rsqrt(jnp.sum(x * x, axis=-1, keepdims=True) * (1.0 / width) + EPS)


def _silu(x):
    return x * jax.nn.sigmoid(x)


def _mod_kernel(cond_ref, w_ref, b_ref, o_ref):
    x = _silu(cond_ref[...]).astype(BF16)
    o_ref[0] = jnp.dot(x, w_ref[0].astype(BF16), preferred_element_type=F32) + b_ref[0]


def _modulation(cond, w_mod, b_mod, tn=1024):
    depth, d, n = w_mod.shape
    return pl.pallas_call(
        _mod_kernel,
        grid=(depth, n // tn),
        in_specs=[
            pl.BlockSpec((COND_ROWS, d), lambda l, j: (0, 0)),
            pl.BlockSpec((1, d, tn), lambda l, j: (l, 0, j)),
            pl.BlockSpec((1, 1, tn), lambda l, j: (l, 0, j)),
        ],
        out_specs=pl.BlockSpec((1, COND_ROWS, tn), lambda l, j: (l, 0, j)),
        out_shape=jax.ShapeDtypeStruct((depth, COND_ROWS, n), F32),
        compiler_params=_params(("arbitrary", "arbitrary")),
        name="modulation",
    )(cond, w_mod, b_mod.reshape(depth, 1, n))


def _norm_mod(x, g, scale1, shift):
    return (_rms(x, D_MODEL) * g) * scale1 + shift


def _inproj_kernel(y_ref, mod_ref, g_ref, w_ref, o_ref, h_scr, *, tm, chunk):
    @pl.when(pl.program_id(1) == 0)
    def _():
        g = g_ref[...]
        shift = mod_ref[0, :, 0:D_MODEL]
        scale1 = 1.0 + mod_ref[0, :, D_MODEL:2 * D_MODEL]

        def body(r0):
            x = y_ref[pl.ds(r0, chunk), :]
            h_scr[pl.ds(r0, chunk), :] = _norm_mod(x, g, scale1, shift).astype(BF16)
        _row_loop(tm, chunk, body)

    o_ref[...] = jnp.dot(h_scr[...], w_ref[...], preferred_element_type=F32)


def _inproj(y, mod, g, w, rows_per_mod, tm=512, tn=1024, chunk=32):
    rows, d = y.shape
    n = w.shape[1]
    tiles_per_mod = rows_per_mod // tm
    return pl.pallas_call(
        functools.partial(_inproj_kernel, tm=tm, chunk=chunk),
        grid=(rows // tm, n // tn),
        in_specs=[
            pl.BlockSpec((tm, d), lambda i, j: (i, 0)),
            pl.BlockSpec((1, 1, 3 * d), lambda i, j: (i // tiles_per_mod, 0, 0)),
            pl.BlockSpec((1, d), lambda i, j: (0, 0)),
            pl.BlockSpec((d, tn), lambda i, j: (0, j)),
        ],
        out_specs=pl.BlockSpec((tm, tn), lambda i, j: (i, j)),
        out_shape=jax.ShapeDtypeStruct((rows, n), F32),
        scratch_shapes=[pltpu.VMEM((tm, d), BF16)],
        compiler_params=_params(("arbitrary", "arbitrary")),
        name="attn_inproj",
    )(y, mod, g, w)


def _rope(x, cos, sin_lo, sin_hi):
    return x * cos + pltpu.roll(x, HEAD_DIM - AXIS_DIM // 2, 1) * sin_lo + pltpu.roll(x, AXIS_DIM // 2, 1) * sin_hi


def _attn_kernel(*refs, t_len, tq, n_ctx, latent):
    if latent:
        (q_ref, k_ref, v_ref, g_ref, ck_ref, cv_ref, qn_ref, kn_ref,
         cosq_ref, sloq_ref, shiq_ref, cosk_ref, slok_ref, shik_ref,
         og_ref, k_scr, v_scr) = refs
    else:
        (q_ref, k_ref, v_ref, g_ref, qn_ref, kn_ref,
         og_ref, ko_ref, vo_ref, k_scr, v_scr) = refs

    @pl.when(pl.program_id(2) == 0)
    def _():
        kn = _rms(k_ref[...], HEAD_DIM) * kn_ref[...]
        v = v_ref[...]
        if latent:
            kn = _rope(kn, cosk_ref[...], slok_ref[...], shik_ref[...])
            k_scr[0:n_ctx, :] = ck_ref[...].astype(BF16)
            v_scr[0:n_ctx, :] = cv_ref[...].astype(BF16)
        else:
            ko_ref[...] = kn
            vo_ref[...] = v
        k_scr[n_ctx:n_ctx + t_len, :] = kn.astype(BF16)
        v_scr[n_ctx:n_ctx + t_len, :] = v.astype(BF16)

    qn = qn_ref[...]
    heads = []
    for h in range(Q_PER_KV):
        qh = _rms(q_ref[:, h * HEAD_DIM:(h + 1) * HEAD_DIM], HEAD_DIM) * qn
        if latent:
            qh = _rope(qh, cosq_ref[...], sloq_ref[...], shiq_ref[...])
        heads.append(qh.astype(BF16))
    q_st = jnp.concatenate(heads, axis=0)
    s = lax.dot_general(q_st, k_scr[...], (((1,), (1,)), ((), ())),
                        preferred_element_type=F32) * (HEAD_DIM ** -0.5)
    e = jnp.exp(s - jnp.max(s, axis=-1, keepdims=True))
    denom = jnp.sum(e, axis=-1, keepdims=True)
    o = jnp.dot(e.astype(BF16), v_scr[...], preferred_element_type=F32) / denom
    for h in range(Q_PER_KV):
        gate = _silu(g_ref[:, h * HEAD_DIM:(h + 1) * HEAD_DIM])
        og_ref[:, h * HEAD_DIM:(h + 1) * HEAD_DIM] = (o[h * tq:(h + 1) * tq, :] * gate).astype(BF16)


def _attention(p, n_batch, t_len, q_norm, k_norm, cache=None, rope=None, tq=256):
    rows = n_batch * t_len
    nq = t_len // tq
    gw = Q_PER_KV * HEAD_DIM
    k_col = ATTN_WIDTH // HEAD_DIM
    v_col = (ATTN_WIDTH + KV_WIDTH) // HEAD_DIM
    g_col = (ATTN_WIDTH + 2 * KV_WIDTH) // gw
    latent = cache is not None
    n_ctx = cache[0].shape[0] // n_batch if latent else 0

    in_specs = [
        pl.BlockSpec((tq, gw), lambda b, kv, qi: (b * nq + qi, kv)),
        pl.BlockSpec((t_len, HEAD_DIM), lambda b, kv, qi: (b, k_col + kv)),
        pl.BlockSpec((t_len, HEAD_DIM), lambda b, kv, qi: (b, v_col + kv)),
        pl.BlockSpec((tq, gw), lambda b, kv, qi: (b * nq + qi, g_col + kv)),
    ]
    args = [p, p, p, p]
    if latent:
        in_specs += [pl.BlockSpec((n_ctx, HEAD_DIM), lambda b, kv, qi: (b, kv))] * 2
        args += list(cache)
    in_specs += [pl.BlockSpec((1, HEAD_DIM), lambda b, kv, qi: (0, 0))] * 2
    args += [q_norm, k_norm]
    og_spec = pl.BlockSpec((tq, gw), lambda b, kv, qi: (b * nq + qi, kv))
    og_shape = jax.ShapeDtypeStruct((rows, ATTN_WIDTH), BF16)
    if latent:
        in_specs += [pl.BlockSpec((tq, HEAD_DIM), lambda b, kv, qi: (qi, 0))] * 3
        in_specs += [pl.BlockSpec((t_len, HEAD_DIM), lambda b, kv, qi: (0, 0))] * 3
        args += list(rope) + list(rope)
        out_specs = og_spec
        out_shape = og_shape
    else:
        kv_spec = pl.BlockSpec((t_len, HEAD_DIM), lambda b, kv, qi: (b, kv))
        kv_shape = jax.ShapeDtypeStruct((rows, KV_WIDTH), F32)
        out_specs = (og_spec, kv_spec, kv_spec)
        out_shape = (og_shape, kv_shape, kv_shape)
    return pl.pallas_call(
        functools.partial(_attn_kernel, t_len=t_len, tq=tq, n_ctx=n_ctx, latent=latent),
        grid=(n_batch, N_KV_HEADS, nq),
        in_specs=in_specs,
        out_specs=out_specs,
        out_shape=out_shape,
        scratch_shapes=[pltpu.VMEM((n_ctx + t_len, HEAD_DIM), BF16)] * 2,
        compiler_params=_params(("arbitrary", "arbitrary", "arbitrary")),
        name="attn_latent" if latent else "attn_context",
    )(*args)


def _rope_tables(n):
    rows = n // GRID_W
    row = jnp.repeat(jnp.arange(rows, dtype=F32), GRID_W)
    col = jnp.tile(jnp.arange(GRID_W, dtype=F32), rows)
    inv = ROPE_THETA ** (-jnp.arange(0, AXIS_DIM, 2, dtype=F32) / AXIS_DIM)
    ar = row[:, None] * inv
    ac = col[:, None] * inv
    ang = jnp.concatenate([ar, ar, ac, ac], axis=-1)
    cos, sin = jnp.cos(ang), jnp.sin(ang)
    first = (jnp.arange(HEAD_DIM) % AXIS_DIM) < AXIS_DIM // 2
    return cos, jnp.where(first, -sin, 0.0), jnp.where(first, 0.0, sin)


def _outproj_kernel(x_ref, w_ref, y_ref, mod_ref, g_ref, o_ref, m_scr, *, tm, chunk):
    m_scr[...] = jnp.dot(x_ref[...], w_ref[...], preferred_element_type=F32)
    g = g_ref[...]
    gate = mod_ref[0, :, 2 * D_MODEL:3 * D_MODEL]

    def body(r0):
        m = m_scr[pl.ds(r0, chunk), :]
        o_ref[pl.ds(r0, chunk), :] = y_ref[pl.ds(r0, chunk), :] + gate * (_rms(m, D_MODEL) * g)
    _row_loop(tm, chunk, body)


def _outproj(x, w, y, mod, g, rows_per_mod, tm=512, chunk=32):
    rows, d = y.shape
    k = x.shape[1]
    tiles_per_mod = rows_per_mod // tm
    return pl.pallas_call(
        functools.partial(_outproj_kernel, tm=tm, chunk=chunk),
        grid=(rows // tm,),
        in_specs=[
            pl.BlockSpec((tm, k), lambda i: (i, 0)),
            pl.BlockSpec((k, d), lambda i: (0, 0)),
            pl.BlockSpec((tm, d), lambda i: (i, 0)),
            pl.BlockSpec((1, 1, 3 * d), lambda i: (i // tiles_per_mod, 0, 0)),
            pl.BlockSpec((1, d), lambda i: (0, 0)),
        ],
        out_specs=pl.BlockSpec((tm, d), lambda i: (i, 0)),
        out_shape=jax.ShapeDtypeStruct((rows, d), F32),
        scratch_shapes=[pltpu.VMEM((tm, d), F32)],
        compiler_params=_params(("arbitrary",)),
        name="attn_outproj",
    )(x, w, y, mod, g)


PERM_ROWS = 32


def _inproj_perm_kernel(y_ref, mod_ref, g_ref, w_ref, o_ref, slab_scr, h_scr, *, n_batch, tt, n_mod):
    @pl.when(pl.program_id(1) == 0)
    def _():
        g = g_ref[...]
        for b in range(n_batch):
            mb = mod_ref[b if n_mod > 1 else 0]
            shift = mb[:, 0:D_MODEL]
            scale1 = 1.0 + mb[:, D_MODEL:2 * D_MODEL]
            for r0 in range(0, tt, PERM_ROWS):
                h = _norm_mod(y_ref[b, r0:r0 + PERM_ROWS, :], g, scale1, shift)
                for c in range(N_SLABS):
                    slab_scr[c, pl.ds(r0 * n_batch + b, PERM_ROWS, stride=n_batch), :] = (
                        h[:, c * LANES:(c + 1) * LANES])
        for c in range(N_SLABS):
            h_scr[:, c * LANES:(c + 1) * LANES] = slab_scr[c].astype(BF16)

    o_ref[...] = jnp.dot(h_scr[...], w_ref[...], preferred_element_type=F32)


def _inproj_perm(y3, mod, g, w, tm=512, tn=1024):
    n_batch, t_len, d = y3.shape
    n = w.shape[1]
    tt = tm // n_batch
    n_mod = mod.shape[0]
    return pl.pallas_call(
        functools.partial(_inproj_perm_kernel, n_batch=n_batch, tt=tt, n_mod=n_mod),
        grid=(t_len // tt, n // tn),
        in_specs=[
            pl.BlockSpec((n_batch, tt, d), lambda i, j: (0, i, 0)),
            pl.BlockSpec((n_mod, 1, 3 * d), lambda i, j: (0, 0, 0)),
            pl.BlockSpec((1, d), lambda i, j: (0, 0)),
            pl.BlockSpec((d, tn), lambda i, j: (0, j)),
        ],
        out_specs=pl.BlockSpec((tm, tn), lambda i, j: (i, j)),
        out_shape=jax.ShapeDtypeStruct((n_batch * t_len, n), F32),
        scratch_shapes=[pltpu.VMEM((N_SLABS, tm, LANES), F32), pltpu.VMEM((tm, d), BF16)],
        compiler_params=_params(("arbitrary", "arbitrary")),
        name="lru_inproj",
    )(y3, mod, g, w)


def _log_sigmoid(x):
    return -(jnp.maximum(-x, 0.0) + jnp.log1p(jnp.exp(-jnp.abs(x))))


def _lru_core_kernel(*refs, n_batch, t_len, chunk, with_final):
    (xb_ref, gb_ref, cw_ref, cb_ref, wa_ref, ba_ref, wx_ref, bx_ref, lam_ref, h0_ref) = refs[:10]
    if with_final:
        y_ref, fin_ref, a_scr, xc_scr, b_scr, hs_scr = refs[10:]
    else:
        y_ref, a_scr, xc_scr, b_scr, hs_scr = refs[10:]
    rows = n_batch * t_len
    c = LRU_BLOCK_DIM
    pad_lo = CONV_LEFT * n_batch
    pad_hi = CONV_RIGHT * n_batch

    a_scr[0:pad_lo, :] = jnp.zeros((pad_lo, c), F32)
    a_scr[pad_lo + rows:pad_lo + rows + pad_hi, :] = jnp.zeros((pad_hi, c), F32)
    for r0 in range(0, rows, chunk):
        a_scr[pad_lo + r0:pad_lo + r0 + chunk, :] = xb_ref[r0:r0 + chunk, :]
    for r0 in range(0, rows, chunk):
        acc = cb_ref[...]
        for j in range(CONV_WIDTH):
            acc = acc + a_scr[r0 + j * n_batch:r0 + j * n_batch + chunk, :] * cw_ref[j:j + 1, :]
        xc_scr[r0:r0 + chunk, :] = acc

    sub = lax.broadcasted_iota(jnp.int32, (8, c), 0)

    for d in range(2):
        log_s = RG_C * _log_sigmoid(lam_ref[d:d + 1, :])
        ba = ba_ref[d:d + 1, :]
        bx = bx_ref[d:d + 1, :]

        def coeffs(r0, d=d, log_s=log_s, ba=ba, bx=bx):
            x = xc_scr[pl.ds(r0, chunk), :]
            x16 = x.astype(BF16)
            r = jax.nn.sigmoid(jnp.dot(x16, wa_ref[d, 0], preferred_element_type=F32) + ba)
            i = jax.nn.sigmoid(jnp.dot(x16, wx_ref[d, 0], preferred_element_type=F32) + bx)
            log_a = r * log_s
            a = jnp.exp(log_a)
            mult = jnp.sqrt(-jnp.tanh(log_a) * (a * a + 1.0))
            a_scr[pl.ds(r0, chunk), :] = a
            b_scr[pl.ds(r0, chunk), :] = (mult * i) * x
        _row_loop(rows, chunk, coeffs)

        h0 = h0_ref[d]
        if n_batch == 16:
            def step(k, h, d=d):
                t = k if d == 0 else t_len - 1 - k
                r = pl.multiple_of(t * 16, 16)
                h = a_scr[pl.ds(r, 16), :] * h + b_scr[pl.ds(r, 16), :]
                if d == 0:
                    hs_scr[pl.ds(r, 16), :] = h
                else:
                    hs_scr[pl.ds(r, 16), :] = hs_scr[pl.ds(r, 16), :] + h
                return h
            h_fin = lax.fori_loop(0, t_len, step, h0, unroll=4)
            if with_final:
                fin_ref[d] = h_fin
        else:
            def step(k, carry, d=d):
                t = k if d == 0 else t_len // 2 - 1 - k
                r = pl.multiple_of(t * 8, 8)
                a = a_scr[pl.ds(r, 8), :]
                b = b_scr[pl.ds(r, 8), :]
                u = a * carry + b
                v = a * pltpu.roll(u, 4, 0) + b
                first = (sub < 4) if d == 0 else (sub >= 4)
                h = jnp.where(first, u, v)
                if d == 0:
                    hs_scr[pl.ds(r, 8), :] = h
                else:
                    hs_scr[pl.ds(r, 8), :] = hs_scr[pl.ds(r, 8), :] + h
                return pltpu.roll(v, 4, 0)
            lax.fori_loop(0, t_len // 2, step, h0, unroll=4)

    def emit(r0):
        y_ref[pl.ds(r0, chunk), :] = (hs_scr[pl.ds(r0, chunk), :] * _silu(gb_ref[pl.ds(r0, chunk), :])).astype(BF16)
    _row_loop(rows, chunk, emit)


def _lru_core(xg, n_batch, t_len, conv_w, conv_b, w_a, b_a, w_x, b_x, lam, h0, with_final, chunk=256):
    rows = n_batch * t_len
    c = LRU_BLOCK_DIM
    h_rows = h0.shape[1]
    pad = (CONV_WIDTH - 1) * n_batch
    in_specs = [
        pl.BlockSpec((rows, c), lambda i: (0, i)),
        pl.BlockSpec((rows, c), lambda i: (0, LRU_BLOCKS + i)),
        pl.BlockSpec((CONV_WIDTH, c), lambda i: (0, i)),
        pl.BlockSpec((1, c), lambda i: (0, i)),
        pl.BlockSpec((2, 1, c, c), lambda i: (0, i, 0, 0)),
        pl.BlockSpec((2, c), lambda i: (0, i)),
        pl.BlockSpec((2, 1, c, c), lambda i: (0, i, 0, 0)),
        pl.BlockSpec((2, c), lambda i: (0, i)),
        pl.BlockSpec((2, c), lambda i: (0, i)),
        pl.BlockSpec((2, h_rows, c), lambda i: (0, 0, i)),
    ]
    y_spec = pl.BlockSpec((rows, c), lambda i: (0, i))
    y_shape = jax.ShapeDtypeStruct((rows, LRU_WIDTH), BF16)
    if with_final:
        out_specs = (y_spec, pl.BlockSpec((2, n_batch, c), lambda i: (0, 0, i)))
        out_shape = (y_shape, jax.ShapeDtypeStruct((2, n_batch, LRU_WIDTH), F32))
    else:
        out_specs = y_spec
        out_shape = y_shape
    return pl.pallas_call(
        functools.partial(_lru_core_kernel, n_batch=n_batch, t_len=t_len, chunk=chunk, with_final=with_final),
        grid=(LRU_BLOCKS,),
        in_specs=in_specs,
        out_specs=out_specs,
        out_shape=out_shape,
        scratch_shapes=[
            pltpu.VMEM((rows + pad, c), F32),
            pltpu.VMEM((rows, c), F32),
            pltpu.VMEM((rows, c), F32),
            pltpu.VMEM((rows, c), F32),
        ],
        compiler_params=_params(("arbitrary",)),
        name="lru_core",
    )(xg, xg, conv_w, conv_b, w_a, b_a, w_x, b_x, lam, h0)


MXU_COLS = 256


def _outproj_perm_kernel(x_ref, w_ref, y_ref, mod_ref, g_ref, o_ref, slab_scr, *, n_batch, tt, n_mod):
    for n0 in range(0, D_MODEL, MXU_COLS):
        m = jnp.dot(x_ref[...], w_ref[:, n0:n0 + MXU_COLS], preferred_element_type=F32)
        for c in range(MXU_COLS // LANES):
            slab_scr[n0 // LANES + c] = m[:, c * LANES:(c + 1) * LANES]
    g = g_ref[...]
    for b in range(n_batch):
        gate = mod_ref[b if n_mod > 1 else 0][:, 2 * D_MODEL:3 * D_MODEL]
        for r0 in range(0, tt, PERM_ROWS):
            m = jnp.concatenate(
                [slab_scr[c, pl.ds(r0 * n_batch + b, PERM_ROWS, stride=n_batch), :] for c in range(N_SLABS)],
                axis=-1)
            o_ref[b, r0:r0 + PERM_ROWS, :] = y_ref[b, r0:r0 + PERM_ROWS, :] + gate * (_rms(m, D_MODEL) * g)


def _outproj_perm(x, w, y3, mod, g, tm=512):
    n_batch, t_len, d = y3.shape
    k = x.shape[1]
    tt = tm // n_batch
    n_mod = mod.shape[0]
    return pl.pallas_call(
        functools.partial(_outproj_perm_kernel, n_batch=n_batch, tt=tt, n_mod=n_mod),
        grid=(t_len // tt,),
        in_specs=[
            pl.BlockSpec((tm, k), lambda i: (i, 0)),
            pl.BlockSpec((k, d), lambda i: (0, 0)),
            pl.BlockSpec((n_batch, tt, d), lambda i: (0, i, 0)),
            pl.BlockSpec((n_mod, 1, 3 * d), lambda i: (0, 0, 0)),
            pl.BlockSpec((1, d), lambda i: (0, 0)),
        ],
        out_specs=pl.BlockSpec((n_batch, tt, d), lambda i: (0, i, 0)),
        out_shape=jax.ShapeDtypeStruct((n_batch, t_len, d), F32),
        scratch_shapes=[pltpu.VMEM((N_SLABS, tm, LANES), F32)],
        compiler_params=_params(("arbitrary",)),
        name="lru_outproj",
    )(x, w, y3, mod, g)


def kernel(x_prompt, x_sample, c, cache_k, cache_v, state_lru, c_ctx, w_mod, b_mod, g_pre, g_post,
           w_in_attn, q_norm, k_norm, w_out_attn, w_in_lru, conv_w, conv_b,
           w_rg_a, b_rg_a, w_rg_x, b_rg_x, lru_lambda, w_out_lru):
    n_p, t_p, d = x_prompt.shape
    n_s, t_s, _ = x_sample.shape
    depth = w_mod.shape[0]
    n_ctx = cache_k.shape[2]

    cond = jnp.concatenate([c_ctx[None, :], c, jnp.zeros((COND_ROWS - 1 - n_s, d), F32)], axis=0)
    mod = _modulation(cond, w_mod, b_mod)

    y_p = x_prompt
    y_s = x_sample
    new_k, new_v, new_h = [], [], []
    for l in range(depth):
        j = l // 2
        mod_p = mod[l, 0:1].reshape(1, 1, 3 * d)
        mod_s = mod[l, 1:1 + n_s].reshape(n_s, 1, 3 * d)
        g_pre_l = g_pre[l].reshape(1, d)
        g_post_l = g_post[l].reshape(1, d)
        if l % 2 == 0:
            w_in = w_in_attn[j].astype(BF16)
            w_out = w_out_attn[j].astype(BF16)
            qn = q_norm[j].reshape(1, HEAD_DIM)
            kn = k_norm[j].reshape(1, HEAD_DIM)
            yp2 = y_p.reshape(n_p * t_p, d)
            ys2 = y_s.reshape(n_s * t_s, d)
            pp = _inproj(yp2, mod_p, g_pre_l, w_in, rows_per_mod=n_p * t_p)
            ps = _inproj(ys2, mod_s, g_pre_l, w_in, rows_per_mod=t_s)
            og_p, k_p, v_p = _attention(pp, n_p, t_p, qn, kn)
            ck = cache_k[:, j].reshape(n_s * n_ctx, KV_WIDTH)
            cv = cache_v[:, j].reshape(n_s * n_ctx, KV_WIDTH)
            og_s = _attention(ps, n_s, t_s, qn, kn, cache=(ck, cv), rope=_rope_tables(t_s))
            y_p = _outproj(og_p, w_out, yp2, mod_p, g_post_l, rows_per_mod=n_p * t_p).reshape(n_p, t_p, d)
            y_s = _outproj(og_s, w_out, ys2, mod_s, g_post_l, rows_per_mod=t_s).reshape(n_s, t_s, d)
            new_k.append(k_p.reshape(n_p, t_p, N_KV_HEADS, HEAD_DIM))
            new_v.append(v_p.reshape(n_p, t_p, N_KV_HEADS, HEAD_DIM))
        else:
            w_in = w_in_lru[j].astype(BF16)
            w_out = w_out_lru[j].astype(BF16)
            w_a = w_rg_a[j].astype(BF16)
            w_x = w_rg_x[j].astype(BF16)
            cb = conv_b[j].reshape(1, LRU_WIDTH)
            lru_args = (conv_w[j], cb, w_a, b_rg_a[j], w_x, b_rg_x[j], lru_lambda[j])
            xg_p = _inproj_perm(y_p, mod_p, g_pre_l, w_in)
            xg_s = _inproj_perm(y_s, mod_s, g_pre_l, w_in)
            h0_p = jnp.zeros((2, n_p, LRU_WIDTH), F32)
            h0_s = jnp.swapaxes(state_lru[:, j], 0, 1)
            h0_s = jnp.concatenate([h0_s, h0_s], axis=1)
            yg_p, fin = _lru_core(xg_p, n_p, t_p, *lru_args, h0_p, with_final=True)
            yg_s = _lru_core(xg_s, n_s, t_s, *lru_args, h0_s, with_final=False)
            y_p = _outproj_perm(yg_p, w_out, y_p, mod_p, g_post_l)
            y_s = _outproj_perm(yg_s, w_out, y_s, mod_s, g_post_l)
            new_h.append(jnp.swapaxes(fin, 0, 1))
    new_cache_k = jnp.stack(new_k, axis=1)
    new_cache_v = jnp.stack(new_v, axis=1)
    new_state_lru = jnp.stack(new_h, axis=1)
    return (y_p, y_s, new_cache_k, new_cache_v, new_state_lru)
```

```python
import functools

import jax
import jax.numpy as jnp
from jax import lax
from jax.experimental import pallas as pl
from jax.experimental.pallas import tpu as pltpu

F32 = jnp.float32
BF16 = jnp.bfloat16

D_MODEL = 2048
HEAD_DIM = 128
N_HEADS = 16
N_KV_HEADS = 4
Q_PER_KV = N_HEADS // N_KV_HEADS
ATTN_WIDTH = N_HEADS * HEAD_DIM
KV_WIDTH = N_KV_HEADS * HEAD_DIM
GRID_W = 64
AXIS_DIM = HEAD_DIM // 2
ROPE_THETA = 10000.0
LRU_WIDTH = D_MODEL
LRU_BLOCKS = 8
LRU_BLOCK_DIM = LRU_WIDTH // LRU_BLOCKS
CONV_WIDTH = 4
CONV_LEFT = (CONV_WIDTH - 1) // 2
CONV_RIGHT = CONV_WIDTH - 1 - CONV_LEFT
RG_C = 8.0
EPS = 1e-6
LOG2_E = 1.4426950408889634

LANES = 128
N_SLABS = D_MODEL // LANES
COND_ROWS = 8
VMEM_LIMIT = 56 * 1024 * 1024


def _params(sem):
    return pltpu.CompilerParams(dimension_semantics=sem, vmem_limit_bytes=VMEM_LIMIT)


def _row_loop(n_rows, chunk, body, unroll=1):
    def step(i, carry):
        body(pl.multiple_of(i * chunk, chunk))
        return carry
    lax.fori_loop(0, n_rows // chunk, step, 0, unroll=unroll)


def _rms(x, width):
    return x * lax.rsqrt(jnp.sum(x * x, axis=-1, keepdims=True) * (1.0 / width) + EPS)


def _sigmoid(x):
    return 0.5 * jnp.tanh(0.5 * x) + 0.5


def _silu(x):
    return x * _sigmoid(x)


def _mod_kernel(cond_ref, w_ref, b_ref, o_ref):
    x = _silu(cond_ref[...]).astype(BF16)
    o_ref[0] = jnp.dot(x, w_ref[0].astype(BF16), preferred_element_type=F32) + b_ref[0]


def _modulation(cond, w_mod, b_mod, tn=1024):
    depth, d, n = w_mod.shape
    return pl.pallas_call(
        _mod_kernel,
        grid=(depth, n // tn),
        in_specs=[
            pl.BlockSpec((COND_ROWS, d), lambda l, j: (0, 0)),
            pl.BlockSpec((1, d, tn), lambda l, j: (l, 0, j)),
            pl.BlockSpec((1, 1, tn), lambda l, j: (l, 0, j)),
        ],
        out_specs=pl.BlockSpec((1, COND_ROWS, tn), lambda l, j: (l, 0, j)),
        out_shape=jax.ShapeDtypeStruct((depth, COND_ROWS, n), F32),
        compiler_params=_params(("arbitrary", "arbitrary")),
        name="modulation",
    )(cond, w_mod, b_mod.reshape(depth, 1, n))


def _norm_mod(x, g, scale1, shift):
    return (_rms(x, D_MODEL) * g) * scale1 + shift


def _inproj_kernel(y_ref, mod_ref, g_ref, w_ref, o_ref, h_scr, *, tm, chunk):
    @pl.when(pl.program_id(1) == 0)
    def _():
        g = g_ref[...]
        shift = mod_ref[0, :, 0:D_MODEL]
        scale1 = 1.0 + mod_ref[0, :, D_MODEL:2 * D_MODEL]

        def body(r0):
            x = y_ref[pl.ds(r0, chunk), :]
            h_scr[pl.ds(r0, chunk), :] = _norm_mod(x, g, scale1, shift).astype(BF16)
        _row_loop(tm, chunk, body)

    o_ref[...] = jnp.dot(h_scr[...], w_ref[...], preferred_element_type=F32)


def _inproj(y, mod, g, w, rows_per_mod, tm=1024, tn=1024, chunk=32):
    rows, d = y.shape
    n = w.shape[1]
    tiles_per_mod = rows_per_mod // tm
    return pl.pallas_call(
        functools.partial(_inproj_kernel, tm=tm, chunk=chunk),
        grid=(rows // tm, n // tn),
        in_specs=[
            pl.BlockSpec((tm, d), lambda i, j: (i, 0)),
            pl.BlockSpec((1, 1, 3 * d), lambda i, j: (i // tiles_per_mod, 0, 0)),
            pl.BlockSpec((1, d), lambda i, j: (0, 0)),
            pl.BlockSpec((d, tn), lambda i, j: (0, j)),
        ],
        out_specs=pl.BlockSpec((tm, tn), lambda i, j: (i, j)),
        out_shape=jax.ShapeDtypeStruct((rows, n), F32),
        scratch_shapes=[pltpu.VMEM((tm, d), BF16)],
        compiler_params=_params(("arbitrary", "arbitrary")),
        name="attn_inproj",
    )(y, mod, g, w)


ATTN_SUB_ROWS = 256


def _rope(x, cos, sin_lo, sin_hi):
    return x * cos + pltpu.roll(x, HEAD_DIM - AXIS_DIM // 2, 1) * sin_lo + pltpu.roll(x, AXIS_DIM // 2, 1) * sin_hi


def _attn_kernel(*refs, t_len, tq, n_ctx, latent):
    if latent:
        (q_ref, k_ref, v_ref, g_ref, ck_ref, cv_ref, qn_ref, kn_ref,
         cosq_ref, sloq_ref, shiq_ref, cosk_ref, slok_ref, shik_ref,
         og_ref, k_scr, v_scr) = refs
    else:
        (q_ref, k_ref, v_ref, g_ref, qn_ref, kn_ref,
         og_ref, ko_ref, vo_ref, k_scr, v_scr) = refs

    @pl.when(pl.program_id(2) == 0)
    def _():
        kn = _rms(k_ref[...], HEAD_DIM) * kn_ref[...]
        v = v_ref[...]
        if latent:
            kn = _rope(kn, cosk_ref[...], slok_ref[...], shik_ref[...])
            k_scr[0:n_ctx, :] = ck_ref[...].astype(BF16)
            v_scr[0:n_ctx, :] = cv_ref[...].astype(BF16)
        else:
            ko_ref[...] = kn
            vo_ref[...] = v
        k_scr[n_ctx:n_ctx + t_len, :] = kn.astype(BF16)
        v_scr[n_ctx:n_ctx + t_len, :] = v.astype(BF16)

    qn = qn_ref[...]
    q_scale = (HEAD_DIM ** -0.5) * LOG2_E
    chains = [(slice(r0, r0 + ATTN_SUB_ROWS), slice(h * HEAD_DIM, (h + 1) * HEAD_DIM))
              for r0 in range(0, tq, ATTN_SUB_ROWS) for h in range(Q_PER_KV)]

    def scores(rows, cols):
        qh = _rms(q_ref[rows, cols], HEAD_DIM) * qn
        if latent:
            qh = _rope(qh, cosq_ref[rows, :], sloq_ref[rows, :], shiq_ref[rows, :])
        qh = (qh * q_scale).astype(BF16)
        return lax.dot_general(qh, k_scr[...], (((1,), (1,)), ((), ())), preferred_element_type=F32)

    s_next = scores(*chains[0])
    for i, (rows, cols) in enumerate(chains):
        s = s_next
        if i + 1 < len(chains):
            s_next = scores(*chains[i + 1])
        e = jnp.exp2(s - jnp.max(s, axis=-1, keepdims=True))
        denom = jnp.sum(e, axis=-1, keepdims=True)
        o = jnp.dot(e.astype(BF16), v_scr[...], preferred_element_type=F32) / denom
        og_ref[rows, cols] = (o * _silu(g_ref[rows, cols])).astype(BF16)


def _attention(p, n_batch, t_len, q_norm, k_norm, cache=None, rope=None, tq=512):
    tq = min(tq, t_len)
    rows = n_batch * t_len
    nq = t_len // tq
    gw = Q_PER_KV * HEAD_DIM
    k_col = ATTN_WIDTH // HEAD_DIM
    v_col = (ATTN_WIDTH + KV_WIDTH) // HEAD_DIM
    g_col = (ATTN_WIDTH + 2 * KV_WIDTH) // gw
    latent = cache is not None
    n_ctx = cache[0].shape[0] // n_batch if latent else 0

    in_specs = [
        pl.BlockSpec((tq, gw), lambda b, kv, qi: (b * nq + qi, kv)),
        pl.BlockSpec((t_len, HEAD_DIM), lambda b, kv, qi: (b, k_col + kv)),
        pl.BlockSpec((t_len, HEAD_DIM), lambda b, kv, qi: (b, v_col + kv)),
        pl.BlockSpec((tq, gw), lambda b, kv, qi: (b * nq + qi, g_col + kv)),
    ]
    args = [p, p, p, p]
    if latent:
        in_specs += [pl.BlockSpec((n_ctx, HEAD_DIM), lambda b, kv, qi: (b, kv))] * 2
        args += list(cache)
    in_specs += [pl.BlockSpec((1, HEAD_DIM), lambda b, kv, qi: (0, 0))] * 2
    args += [q_norm, k_norm]
    og_spec = pl.BlockSpec((tq, gw), lambda b, kv, qi: (b * nq + qi, kv))
    og_shape = jax.ShapeDtypeStruct((rows, ATTN_WIDTH), BF16)
    if latent:
        in_specs += [pl.BlockSpec((tq, HEAD_DIM), lambda b, kv, qi: (qi, 0))] * 3
        in_specs += [pl.BlockSpec((t_len, HEAD_DIM), lambda b, kv, qi: (0, 0))] * 3
        args += list(rope) + list(rope)
        out_specs = og_spec
        out_shape = og_shape
    else:
        kv_spec = pl.BlockSpec((t_len, HEAD_DIM), lambda b, kv, qi: (b, kv))
        kv_shape = jax.ShapeDtypeStruct((rows, KV_WIDTH), F32)
        out_specs = (og_spec, kv_spec, kv_spec)
        out_shape = (og_shape, kv_shape, kv_shape)
    return pl.pallas_call(
        functools.partial(_attn_kernel, t_len=t_len, tq=tq, n_ctx=n_ctx, latent=latent),
        grid=(n_batch, N_KV_HEADS, nq),
        in_specs=in_specs,
        out_specs=out_specs,
        out_shape=out_shape,
        scratch_shapes=[pltpu.VMEM((n_ctx + t_len, HEAD_DIM), BF16)] * 2,
        compiler_params=_params(("arbitrary", "arbitrary", "arbitrary")),
        name="attn_latent" if latent else "attn_context",
    )(*args)


def _rope_tables(n):
    rows = n // GRID_W
    row = jnp.repeat(jnp.arange(rows, dtype=F32), GRID_W)
    col = jnp.tile(jnp.arange(GRID_W, dtype=F32), rows)
    inv = ROPE_THETA ** (-jnp.arange(0, AXIS_DIM, 2, dtype=F32) / AXIS_DIM)
    ar = row[:, None] * inv
    ac = col[:, None] * inv
    ang = jnp.concatenate([ar, ar, ac, ac], axis=-1)
    cos, sin = jnp.cos(ang), jnp.sin(ang)
    first = (jnp.arange(HEAD_DIM) % AXIS_DIM) < AXIS_DIM // 2
    return cos, jnp.where(first, -sin, 0.0), jnp.where(first, 0.0, sin)


def _outproj_kernel(x_ref, w_ref, y_ref, mod_ref, g_ref, o_ref, m_scr, *, tm, chunk):
    m_scr[...] = jnp.dot(x_ref[...], w_ref[...], preferred_element_type=F32)
    g = g_ref[...]
    gate = mod_ref[0, :, 2 * D_MODEL:3 * D_MODEL]

    def body(r0):
        m = m_scr[pl.ds(r0, chunk), :]
        o_ref[pl.ds(r0, chunk), :] = y_ref[pl.ds(r0, chunk), :] + gate * (_rms(m, D_MODEL) * g)
    _row_loop(tm, chunk, body, unroll=2)


def _outproj(x, w, y, mod, g, rows_per_mod, tm=512, chunk=32):
    rows, d = y.shape
    k = x.shape[1]
    tiles_per_mod = rows_per_mod // tm
    return pl.pallas_call(
        functools.partial(_outproj_kernel, tm=tm, chunk=chunk),
        grid=(rows // tm,),
        in_specs=[
            pl.BlockSpec((tm, k), lambda i: (i, 0)),
            pl.BlockSpec((k, d), lambda i: (0, 0)),
            pl.BlockSpec((tm, d), lambda i: (i, 0)),
            pl.BlockSpec((1, 1, 3 * d), lambda i: (i // tiles_per_mod, 0, 0)),
            pl.BlockSpec((1, d), lambda i: (0, 0)),
        ],
        out_specs=pl.BlockSpec((tm, d), lambda i: (i, 0)),
        out_shape=jax.ShapeDtypeStruct((rows, d), F32),
        scratch_shapes=[pltpu.VMEM((tm, d), F32)],
        compiler_params=_params(("arbitrary",)),
        name="attn_outproj",
    )(x, w, y, mod, g)


PERM_ROWS = 32


def _inproj_perm_kernel(y_ref, mod_ref, g_ref, w_ref, o_ref, slab_scr, h_scr, *, n_batch, tt, n_mod):
    @pl.when(pl.program_id(1) == 0)
    def _():
        g = g_ref[...]
        for b in range(n_batch):
            mb = mod_ref[b if n_mod > 1 else 0]
            shift = mb[:, 0:D_MODEL]
            scale1 = 1.0 + mb[:, D_MODEL:2 * D_MODEL]
            for r0 in range(0, tt, PERM_ROWS):
                h = _norm_mod(y_ref[b, r0:r0 + PERM_ROWS, :], g, scale1, shift)
                for c in range(N_SLABS):
                    slab_scr[c, pl.ds(r0 * n_batch + b, PERM_ROWS, stride=n_batch), :] = (
                        h[:, c * LANES:(c + 1) * LANES])
        for c in range(N_SLABS):
            h_scr[:, c * LANES:(c + 1) * LANES] = slab_scr[c].astype(BF16)

    o_ref[...] = jnp.dot(h_scr[...], w_ref[...], preferred_element_type=F32)


def _inproj_perm(y3, mod, g, w, tm=1024, tn=1024):
    n_batch, t_len, d = y3.shape
    n = w.shape[1]
    tt = tm // n_batch
    n_mod = mod.shape[0]
    return pl.pallas_call(
        functools.partial(_inproj_perm_kernel, n_batch=n_batch, tt=tt, n_mod=n_mod),
        grid=(t_len // tt, n // tn),
        in_specs=[
            pl.BlockSpec((n_batch, tt, d), lambda i, j: (0, i, 0)),
            pl.BlockSpec((n_mod, 1, 3 * d), lambda i, j: (0, 0, 0)),
            pl.BlockSpec((1, d), lambda i, j: (0, 0)),
            pl.BlockSpec((d, tn), lambda i, j: (0, j)),
        ],
        out_specs=pl.BlockSpec((tm, tn), lambda i, j: (i, j)),
        out_shape=jax.ShapeDtypeStruct((n_batch * t_len, n), F32),
        scratch_shapes=[pltpu.VMEM((N_SLABS, tm, LANES), F32), pltpu.VMEM((tm, d), BF16)],
        compiler_params=_params(("arbitrary", "arbitrary")),
        name="lru_inproj",
    )(y3, mod, g, w)


SCAN_STEPS = 8


def _log_sigmoid(x):
    return -(jnp.maximum(-x, 0.0) + jnp.log1p(jnp.exp(-jnp.abs(x))))


def _lru_core_kernel(*refs, n_batch, t_len, chunk, with_final):
    (xb_ref, gb_ref, cw_ref, cb_ref, wa_ref, ba_ref, wx_ref, bx_ref, lam_ref, h0_ref) = refs[:10]
    if with_final:
        y_ref, fin_ref, a_scr, b_scr, xc_scr, hb_scr, w16_scr = refs[10:]
    else:
        y_ref, a_scr, b_scr, xc_scr, hb_scr, w16_scr = refs[10:]
    rows = n_batch * t_len
    c = LRU_BLOCK_DIM
    pad_lo = CONV_LEFT * n_batch
    pad_hi = CONV_RIGHT * n_batch

    a_scr[1, 0:pad_lo, :] = jnp.zeros((pad_lo, c), F32)
    a_scr[1, pad_lo + rows:pad_lo + rows + pad_hi, :] = jnp.zeros((pad_hi, c), F32)
    for r0 in range(0, rows, chunk):
        a_scr[1, pad_lo + r0:pad_lo + r0 + chunk, :] = xb_ref[r0:r0 + chunk, :]
    for r0 in range(0, rows, chunk):
        acc = cb_ref[...]
        for j in range(CONV_WIDTH):
            acc = acc + a_scr[1, r0 + j * n_batch:r0 + j * n_batch + chunk, :] * cw_ref[j:j + 1, :]
        xc_scr[r0:r0 + chunk, :] = acc

    for d in range(2):
        w16_scr[2 * d] = (0.5 * wa_ref[d, 0]).astype(BF16)
        w16_scr[2 * d + 1] = (0.5 * wx_ref[d, 0]).astype(BF16)
    half_ba = [0.5 * ba_ref[d:d + 1, :] for d in range(2)]
    half_bx = [0.5 * bx_ref[d:d + 1, :] for d in range(2)]
    neg_log_s = [-(RG_C * _log_sigmoid(lam_ref[d:d + 1, :])) for d in range(2)]
    exp2_s = [-LOG2_E * neg_log_s[d] for d in range(2)]

    def coeffs(r0):
        x = xc_scr[pl.ds(r0, chunk), :]
        x16 = x.astype(BF16)
        for d in range(2):
            r = 0.5 * jnp.tanh(jnp.dot(x16, w16_scr[2 * d], preferred_element_type=F32) + half_ba[d]) + 0.5
            i = 0.5 * jnp.tanh(jnp.dot(x16, w16_scr[2 * d + 1], preferred_element_type=F32) + half_bx[d]) + 0.5
            a = jnp.exp2(r * exp2_s[d])
            m2 = jnp.tanh(r * neg_log_s[d]) * (a * a + 1.0)
            mult = jnp.where(m2 > 0.0, m2 * lax.rsqrt(m2), 0.0)
            a_scr[d, pl.ds(r0, chunk), :] = a
            b_scr[d, pl.ds(r0, chunk), :] = (mult * i) * x
    _row_loop(rows, chunk, coeffs, unroll=2)

    hf_scr = xc_scr
    if n_batch == 16:
        blk = SCAN_STEPS * 16

        def steps(kb, carry):
            hf, hb = carry
            base_f = pl.multiple_of(kb * blk, blk)
            base_b = pl.multiple_of(rows - blk - kb * blk, blk)
            af = a_scr[0, pl.ds(base_f, blk), :]
            bf = b_scr[0, pl.ds(base_f, blk), :]
            ab = a_scr[1, pl.ds(base_b, blk), :]
            bb = b_scr[1, pl.ds(base_b, blk), :]
            hfs, hbs = [], []
            for j in range(SCAN_STEPS):
                rf = slice(16 * j, 16 * (j + 1))
                rb = slice(blk - 16 * (j + 1), blk - 16 * j)
                hf = af[rf] * hf + bf[rf]
                hb = ab[rb] * hb + bb[rb]
                hfs.append(hf)
                hbs.append(hb)
            hf_scr[pl.ds(base_f, blk), :] = jnp.concatenate(hfs, axis=0)
            hb_scr[pl.ds(base_b, blk), :] = jnp.concatenate(hbs[::-1], axis=0)
            return hf, hb
        hf, hb = lax.fori_loop(0, t_len // SCAN_STEPS, steps, (h0_ref[0], h0_ref[1]))
        if with_final:
            fin_ref[0] = hf
            fin_ref[1] = hb
    else:
        lower = lax.broadcasted_iota(jnp.int32, (8, c), 0) < 4

        blk = SCAN_STEPS * 8

        def steps(kb, carry):
            cf, cb = carry
            base_f = pl.multiple_of(kb * blk, blk)
            base_b = pl.multiple_of(rows - blk - kb * blk, blk)
            for j in range(SCAN_STEPS):
                rf = pl.ds(base_f + 8 * j, 8)
                rb = pl.ds(base_b + 8 * (SCAN_STEPS - 1 - j), 8)
                af = a_scr[0, rf, :]
                bf = b_scr[0, rf, :]
                ab = a_scr[1, rb, :]
                bb = b_scr[1, rb, :]
                uf = af * cf + bf
                ub = ab * cb + bb
                cf = pltpu.roll(af, 4, 0) * uf + pltpu.roll(bf, 4, 0)
                cb = pltpu.roll(ab, 4, 0) * ub + pltpu.roll(bb, 4, 0)
                hf_scr[rf, :] = jnp.where(lower, uf, pltpu.roll(cf, 4, 0))
                hb_scr[rb, :] = jnp.where(lower, pltpu.roll(cb, 4, 0), ub)
            return cf, cb
        lax.fori_loop(0, rows // blk, steps, (h0_ref[0], h0_ref[1]))

    def emit(r0):
        h = hf_scr[pl.ds(r0, chunk), :] + hb_scr[pl.ds(r0, chunk), :]
        y_ref[pl.ds(r0, chunk), :] = (h * _silu(gb_ref[pl.ds(r0, chunk), :])).astype(BF16)
    _row_loop(rows, chunk, emit)


def _lru_core(xg, n_batch, t_len, conv_w, conv_b, w_a, b_a, w_x, b_x, lam, h0, with_final, chunk=256):
    rows = n_batch * t_len
    c = LRU_BLOCK_DIM
    h_rows = h0.shape[1]
    pad = (CONV_WIDTH - 1) * n_batch
    in_specs = [
        pl.BlockSpec((rows, c), lambda i: (0, i)),
        pl.BlockSpec((rows, c), lambda i: (0, LRU_BLOCKS + i)),
        pl.BlockSpec((CONV_WIDTH, c), lambda i: (0, i)),
        pl.BlockSpec((1, c), lambda i: (0, i)),
        pl.BlockSpec((2, 1, c, c), lambda i: (0, i, 0, 0)),
        pl.BlockSpec((2, c), lambda i: (0, i)),
        pl.BlockSpec((2, 1, c, c), lambda i: (0, i, 0, 0)),
        pl.BlockSpec((2, c), lambda i: (0, i)),
        pl.BlockSpec((2, c), lambda i: (0, i)),
        pl.BlockSpec((2, h_rows, c), lambda i: (0, 0, i)),
    ]
    y_spec = pl.BlockSpec((rows, c), lambda i: (0, i))
    y_shape = jax.ShapeDtypeStruct((rows, LRU_WIDTH), BF16)
    if with_final:
        out_specs = (y_spec, pl.BlockSpec((2, n_batch, c), lambda i: (0, 0, i)))
        out_shape = (y_shape, jax.ShapeDtypeStruct((2, n_batch, LRU_WIDTH), F32))
    else:
        out_specs = y_spec
        out_shape = y_shape
    return pl.pallas_call(
        functools.partial(_lru_core_kernel, n_batch=n_batch, t_len=t_len, chunk=chunk, with_final=with_final),
        grid=(LRU_BLOCKS,),
        in_specs=in_specs,
        out_specs=out_specs,
        out_shape=out_shape,
        scratch_shapes=[
            pltpu.VMEM((2, rows + pad, c), F32),
            pltpu.VMEM((2, rows, c), F32),
            pltpu.VMEM((rows, c), F32),
            pltpu.VMEM((rows, c), F32),
            pltpu.VMEM((4, c, c), BF16),
        ],
        compiler_params=_params(("arbitrary",)),
        name="lru_core",
    )(xg, xg, conv_w, conv_b, w_a, b_a, w_x, b_x, lam, h0)


MXU_COLS = 256


def _outproj_perm_kernel(x_ref, w_ref, y_ref, mod_ref, g_ref, o_ref, slab_scr, *, n_batch, tt, n_mod):
    for n0 in range(0, D_MODEL, MXU_COLS):
        m = jnp.dot(x_ref[...], w_ref[:, n0:n0 + MXU_COLS], preferred_element_type=F32)
        for c in range(MXU_COLS // LANES):
            slab_scr[n0 // LANES + c] = m[:, c * LANES:(c + 1) * LANES]
    g = g_ref[...]
    for b in range(n_batch):
        gate = mod_ref[b if n_mod > 1 else 0][:, 2 * D_MODEL:3 * D_MODEL]
        for r0 in range(0, tt, PERM_ROWS):
            m = jnp.concatenate(
                [slab_scr[c, pl.ds(r0 * n_batch + b, PERM_ROWS, stride=n_batch), :] for c in range(N_SLABS)],
                axis=-1)
            o_ref[b, r0:r0 + PERM_ROWS, :] = y_ref[b, r0:r0 + PERM_ROWS, :] + gate * (_rms(m, D_MODEL) * g)


def _outproj_perm(x, w, y3, mod, g, tm=512):
    n_batch, t_len, d = y3.shape
    k = x.shape[1]
    tt = tm // n_batch
    n_mod = mod.shape[0]
    return pl.pallas_call(
        functools.partial(_outproj_perm_kernel, n_batch=n_batch, tt=tt, n_mod=n_mod),
        grid=(t_len // tt,),
        in_specs=[
            pl.BlockSpec((tm, k), lambda i: (i, 0)),
            pl.BlockSpec((k, d), lambda i: (0, 0)),
            pl.BlockSpec((n_batch, tt, d), lambda i: (0, i, 0)),
            pl.BlockSpec((n_mod, 1, 3 * d), lambda i: (0, 0, 0)),
            pl.BlockSpec((1, d), lambda i: (0, 0)),
        ],
        out_specs=pl.BlockSpec((n_batch, tt, d), lambda i: (0, i, 0)),
        out_shape=jax.ShapeDtypeStruct((n_batch, t_len, d), F32),
        scratch_shapes=[pltpu.VMEM((N_SLABS, tm, LANES), F32)],
        compiler_params=_params(("arbitrary",)),
        name="lru_outproj",
    )(x, w, y3, mod, g)


def kernel(x_prompt, x_sample, c, cache_k, cache_v, state_lru, c_ctx, w_mod, b_mod, g_pre, g_post,
           w_in_attn, q_norm, k_norm, w_out_attn, w_in_lru, conv_w, conv_b,
           w_rg_a, b_rg_a, w_rg_x, b_rg_x, lru_lambda, w_out_lru):
    n_p, t_p, d = x_prompt.shape
    n_s, t_s, _ = x_sample.shape
    depth = w_mod.shape[0]
    n_ctx = cache_k.shape[2]

    cond = jnp.concatenate([c_ctx[None, :], c, jnp.zeros((COND_ROWS - 1 - n_s, d), F32)], axis=0)
    mod = _modulation(cond, w_mod, b_mod)

    y_p = x_prompt
    y_s = x_sample
    new_k, new_v, new_h = [], [], []
    for l in range(depth):
        j = l // 2
        mod_p = mod[l, 0:1].reshape(1, 1, 3 * d)
        mod_s = mod[l, 1:1 + n_s].reshape(n_s, 1, 3 * d)
        g_pre_l = g_pre[l].reshape(1, d)
        g_post_l = g_post[l].reshape(1, d)
        if l % 2 == 0:
            w_in = w_in_attn[j].astype(BF16)
            w_out = w_out_attn[j].astype(BF16)
            qn = q_norm[j].reshape(1, HEAD_DIM)
            kn = k_norm[j].reshape(1, HEAD_DIM)
            yp2 = y_p.reshape(n_p * t_p, d)
            ys2 = y_s.reshape(n_s * t_s, d)
            pp = _inproj(yp2, mod_p, g_pre_l, w_in, rows_per_mod=n_p * t_p)
            ps = _inproj(ys2, mod_s, g_pre_l, w_in, rows_per_mod=t_s)
            og_p, k_p, v_p = _attention(pp, n_p, t_p, qn, kn)
            ck = cache_k[:, j].reshape(n_s * n_ctx, KV_WIDTH)
            cv = cache_v[:, j].reshape(n_s * n_ctx, KV_WIDTH)
            og_s = _attention(ps, n_s, t_s, qn, kn, cache=(ck, cv), rope=_rope_tables(t_s))
            y_p = _outproj(og_p, w_out, yp2, mod_p, g_post_l, rows_per_mod=n_p * t_p).reshape(n_p, t_p, d)
            y_s = _outproj(og_s, w_out, ys2, mod_s, g_post_l, rows_per_mod=t_s).reshape(n_s, t_s, d)
            new_k.append(k_p.reshape(n_p, t_p, N_KV_HEADS, HEAD_DIM))
            new_v.append(v_p.reshape(n_p, t_p, N_KV_HEADS, HEAD_DIM))
        else:
            w_in = w_in_lru[j].astype(BF16)
            w_out = w_out_lru[j].astype(BF16)
            cb = conv_b[j].reshape(1, LRU_WIDTH)
            lru_args = (conv_w[j], cb, w_rg_a[j], b_rg_a[j], w_rg_x[j], b_rg_x[j], lru_lambda[j])
            xg_p = _inproj_perm(y_p, mod_p, g_pre_l, w_in)
            xg_s = _inproj_perm(y_s, mod_s, g_pre_l, w_in)
            h0_p = jnp.zeros((2, n_p, LRU_WIDTH), F32)
            h0_s = jnp.swapaxes(state_lru[:, j], 0, 1)
            h0_s = jnp.concatenate([h0_s, h0_s], axis=1)
            yg_p, fin = _lru_core(xg_p, n_p, t_p, *lru_args, h0_p, with_final=True)
            yg_s = _lru_core(xg_s, n_s, t_s, *lru_args, h0_s, with_final=False)
            y_p = _outproj_perm(yg_p, w_out, y_p, mod_p, g_post_l)
            y_s = _outproj_perm(yg_s, w_out, y_s, mod_s, g_post_l)
            new_h.append(jnp.swapaxes(fin, 0, 1))
    new_cache_k = jnp.stack(new_k, axis=1)
    new_cache_v = jnp.stack(new_v, axis=1)
    new_state_lru = jnp.stack(new_h, axis=1)
    return (y_p, y_s, new_cache_k, new_cache_v, new_state_lru)
```

```python
import functools

import jax
import jax.numpy as jnp
from jax import lax
from jax.experimental import pallas as pl
from jax.experimental.pallas import tpu as pltpu

F32 = jnp.float32
BF16 = jnp.bfloat16

D_MODEL = 2048
HEAD_DIM = 128
N_HEADS = 16
N_KV_HEADS = 4
Q_PER_KV = N_HEADS // N_KV_HEADS
ATTN_WIDTH = N_HEADS * HEAD_DIM
KV_WIDTH = N_KV_HEADS * HEAD_DIM
GRID_W = 64
AXIS_DIM = HEAD_DIM // 2
ROPE_THETA = 10000.0
LRU_WIDTH = D_MODEL
LRU_BLOCKS = 8
LRU_BLOCK_DIM = LRU_WIDTH // LRU_BLOCKS
CONV_WIDTH = 4
CONV_LEFT = (CONV_WIDTH - 1) // 2
CONV_RIGHT = CONV_WIDTH - 1 - CONV_LEFT
RG_C = 8.0
EPS = 1e-6
LOG2_E = 1.4426950408889634

LANES = 128
N_SLABS = D_MODEL // LANES
COND_ROWS = 8
VMEM_LIMIT = 56 * 1024 * 1024


def _params(sem):
    return pltpu.CompilerParams(dimension_semantics=sem, vmem_limit_bytes=VMEM_LIMIT)


def _row_loop(n_rows, chunk, body, unroll=1):
    def step(i, carry):
        body(pl.multiple_of(i * chunk, chunk))
        return carry
    lax.fori_loop(0, n_rows // chunk, step, 0, unroll=unroll)


def _rms(x, width):
    return x * lax.rsqrt(jnp.sum(x * x, axis=-1, keepdims=True) * (1.0 / width) + EPS)


def _sigmoid(x):
    return 0.5 * jnp.tanh(0.5 * x) + 0.5


def _silu(x):
    return x * _sigmoid(x)


def _mod_kernel(cond_ref, w_ref, b_ref, o_ref):
    x = _silu(cond_ref[...]).astype(BF16)
    o_ref[0] = jnp.dot(x, w_ref[0].astype(BF16), preferred_element_type=F32) + b_ref[0]


def _modulation(cond, w_mod, b_mod, tn=1024):
    depth, d, n = w_mod.shape
    return pl.pallas_call(
        _mod_kernel,
        grid=(depth, n // tn),
        in_specs=[
            pl.BlockSpec((COND_ROWS, d), lambda l, j: (0, 0)),
            pl.BlockSpec((1, d, tn), lambda l, j: (l, 0, j)),
            pl.BlockSpec((1, 1, tn), lambda l, j: (l, 0, j)),
        ],
        out_specs=pl.BlockSpec((1, COND_ROWS, tn), lambda l, j: (l, 0, j)),
        out_shape=jax.ShapeDtypeStruct((depth, COND_ROWS, n), F32),
        compiler_params=_params(("arbitrary", "arbitrary")),
        name="modulation",
    )(cond, w_mod, b_mod.reshape(depth, 1, n))


def _norm_mod(x, g, scale1, shift):
    return (_rms(x, D_MODEL) * g) * scale1 + shift


def _inproj_kernel(y_ref, mod_ref, g_ref, w_ref, o_ref, h_scr, *, tm, chunk):
    @pl.when(pl.program_id(1) == 0)
    def _():
        g = g_ref[...]
        shift = mod_ref[0, :, 0:D_MODEL]
        scale1 = 1.0 + mod_ref[0, :, D_MODEL:2 * D_MODEL]

        def body(r0):
            x = y_ref[pl.ds(r0, chunk), :]
            h_scr[pl.ds(r0, chunk), :] = _norm_mod(x, g, scale1, shift).astype(BF16)
        _row_loop(tm, chunk, body, unroll=4)

    o_ref[...] = jnp.dot(h_scr[...], w_ref[...], preferred_element_type=F32)


def _inproj(y, mod, g, w, rows_per_mod, tm=1024, tn=1024, chunk=32):
    rows, d = y.shape
    n = w.shape[1]
    tiles_per_mod = rows_per_mod // tm
    return pl.pallas_call(
        functools.partial(_inproj_kernel, tm=tm, chunk=chunk),
        grid=(rows // tm, n // tn),
        in_specs=[
            pl.BlockSpec((tm, d), lambda i, j: (i, 0)),
            pl.BlockSpec((1, 1, 3 * d), lambda i, j: (i // tiles_per_mod, 0, 0)),
            pl.BlockSpec((1, d), lambda i, j: (0, 0)),
            pl.BlockSpec((d, tn), lambda i, j: (0, j)),
        ],
        out_specs=pl.BlockSpec((tm, tn), lambda i, j: (i, j)),
        out_shape=jax.ShapeDtypeStruct((rows, n), F32),
        scratch_shapes=[pltpu.VMEM((tm, d), BF16)],
        compiler_params=_params(("arbitrary", "arbitrary")),
        name="attn_inproj",
    )(y, mod, g, w)


ATTN_SUB_ROWS = 256


def _rope(x, cos, sin_lo, sin_hi):
    return x * cos + pltpu.roll(x, HEAD_DIM - AXIS_DIM // 2, 1) * sin_lo + pltpu.roll(x, AXIS_DIM // 2, 1) * sin_hi


def _attn_kernel(*refs, t_len, tq, n_ctx, n_seq, latent):
    if latent:
        (q_ref, k_ref, v_ref, g_ref, ck_ref, cv_ref, qn_ref, kn_ref,
         cosq_ref, sloq_ref, shiq_ref, cosk_ref, slok_ref, shik_ref,
         og_ref, k_scr, v_scr) = refs
    else:
        (q_ref, k_ref, v_ref, g_ref, qn_ref, kn_ref,
         og_ref, ko_ref, vo_ref, k_scr, v_scr) = refs

    @pl.when(pl.program_id(2) == 0)
    def _():
        for s in range(n_seq):
            rows = slice(s * t_len, (s + 1) * t_len)
            kn = _rms(k_ref[rows, :], HEAD_DIM) * kn_ref[...]
            v = v_ref[rows, :]
            if latent:
                kn = _rope(kn, cosk_ref[...], slok_ref[...], shik_ref[...])
                k_scr[s, 0:n_ctx, :] = ck_ref[...].astype(BF16)
                v_scr[s, 0:n_ctx, :] = cv_ref[...].astype(BF16)
            else:
                ko_ref[rows, :] = kn
                vo_ref[rows, :] = v
            k_scr[s, n_ctx:n_ctx + t_len, :] = kn.astype(BF16)
            v_scr[s, n_ctx:n_ctx + t_len, :] = v.astype(BF16)

    qn = qn_ref[...]
    q_scale = (HEAD_DIM ** -0.5) * LOG2_E
    chains = [(s, s * tq + r0, slice(h * HEAD_DIM, (h + 1) * HEAD_DIM))
              for s in range(n_seq) for r0 in range(0, tq, ATTN_SUB_ROWS) for h in range(Q_PER_KV)]

    def scores(s, r0, cols):
        rows = slice(r0, r0 + ATTN_SUB_ROWS)
        qh = _rms(q_ref[rows, cols], HEAD_DIM) * qn
        if latent:
            qh = _rope(qh, cosq_ref[rows, :], sloq_ref[rows, :], shiq_ref[rows, :])
        qh = (qh * q_scale).astype(BF16)
        return lax.dot_general(qh, k_scr[s], (((1,), (1,)), ((), ())), preferred_element_type=F32)

    sc_next = scores(*chains[0])
    for i, (s, r0, cols) in enumerate(chains):
        rows = slice(r0, r0 + ATTN_SUB_ROWS)
        sc = sc_next
        if i + 1 < len(chains):
            sc_next = scores(*chains[i + 1])
        e = jnp.exp2(sc - jnp.max(sc, axis=-1, keepdims=True))
        denom = jnp.sum(e, axis=-1, keepdims=True)
        o = jnp.dot(e.astype(BF16), v_scr[s], preferred_element_type=F32) / denom
        og_ref[rows, cols] = (o * _silu(g_ref[rows, cols])).astype(BF16)


def _attention(p, n_batch, t_len, q_norm, k_norm, cache=None, rope=None, tq=512, n_seq=1):
    tq = min(tq, t_len)
    assert n_seq == 1 or tq == t_len
    rows = n_batch * t_len
    nq = t_len // tq
    gw = Q_PER_KV * HEAD_DIM
    k_col = ATTN_WIDTH // HEAD_DIM
    v_col = (ATTN_WIDTH + KV_WIDTH) // HEAD_DIM
    g_col = (ATTN_WIDTH + 2 * KV_WIDTH) // gw
    latent = cache is not None
    n_ctx = cache[0].shape[0] // n_batch if latent else 0

    in_specs = [
        pl.BlockSpec((n_seq * tq, gw), lambda b, kv, qi: (b * nq + qi, kv)),
        pl.BlockSpec((n_seq * t_len, HEAD_DIM), lambda b, kv, qi: (b, k_col + kv)),
        pl.BlockSpec((n_seq * t_len, HEAD_DIM), lambda b, kv, qi: (b, v_col + kv)),
        pl.BlockSpec((n_seq * tq, gw), lambda b, kv, qi: (b * nq + qi, g_col + kv)),
    ]
    args = [p, p, p, p]
    if latent:
        in_specs += [pl.BlockSpec((n_ctx, HEAD_DIM), lambda b, kv, qi: (b, kv))] * 2
        args += list(cache)
    in_specs += [pl.BlockSpec((1, HEAD_DIM), lambda b, kv, qi: (0, 0))] * 2
    args += [q_norm, k_norm]
    og_spec = pl.BlockSpec((n_seq * tq, gw), lambda b, kv, qi: (b * nq + qi, kv))
    og_shape = jax.ShapeDtypeStruct((rows, ATTN_WIDTH), BF16)
    if latent:
        in_specs += [pl.BlockSpec((tq, HEAD_DIM), lambda b, kv, qi: (qi, 0))] * 3
        in_specs += [pl.BlockSpec((t_len, HEAD_DIM), lambda b, kv, qi: (0, 0))] * 3
        args += list(rope) + list(rope)
        out_specs = og_spec
        out_shape = og_shape
    else:
        kv_spec = pl.BlockSpec((n_seq * t_len, HEAD_DIM), lambda b, kv, qi: (b, kv))
        kv_shape = jax.ShapeDtypeStruct((rows, KV_WIDTH), F32)
        out_specs = (og_spec, kv_spec, kv_spec)
        out_shape = (og_shape, kv_shape, kv_shape)
    return pl.pallas_call(
        functools.partial(_attn_kernel, t_len=t_len, tq=tq, n_ctx=n_ctx, n_seq=n_seq, latent=latent),
        grid=(n_batch // n_seq, N_KV_HEADS, nq),
        in_specs=in_specs,
        out_specs=out_specs,
        out_shape=out_shape,
        scratch_shapes=[pltpu.VMEM((n_seq, n_ctx + t_len, HEAD_DIM), BF16)] * 2,
        compiler_params=_params(("arbitrary", "arbitrary", "arbitrary")),
        name="attn_latent" if latent else "attn_context",
    )(*args)


def _rope_tables(n):
    rows = n // GRID_W
    row = jnp.repeat(jnp.arange(rows, dtype=F32), GRID_W)
    col = jnp.tile(jnp.arange(GRID_W, dtype=F32), rows)
    inv = ROPE_THETA ** (-jnp.arange(0, AXIS_DIM, 2, dtype=F32) / AXIS_DIM)
    ar = row[:, None] * inv
    ac = col[:, None] * inv
    ang = jnp.concatenate([ar, ar, ac, ac], axis=-1)
    cos, sin = jnp.cos(ang), jnp.sin(ang)
    first = (jnp.arange(HEAD_DIM) % AXIS_DIM) < AXIS_DIM // 2
    return cos, jnp.where(first, -sin, 0.0), jnp.where(first, 0.0, sin)


OUT_SUB_ROWS = 256


def _outproj_kernel(x_ref, w_ref, y_ref, mod_ref, g_ref, o_ref, *, tm, chunk):
    g = g_ref[...]
    gate = mod_ref[0, :, 2 * D_MODEL:3 * D_MODEL]
    def project(r0):
        return jnp.dot(x_ref[r0:r0 + OUT_SUB_ROWS, :], w_ref[...], preferred_element_type=F32)

    m_next = project(0)
    for r0 in range(0, tm, OUT_SUB_ROWS):
        m = m_next
        if r0 + OUT_SUB_ROWS < tm:
            m_next = project(r0 + OUT_SUB_ROWS)
        for c0 in range(0, OUT_SUB_ROWS, chunk):
            rows = slice(r0 + c0, r0 + c0 + chunk)
            o_ref[rows, :] = y_ref[rows, :] + gate * (_rms(m[c0:c0 + chunk, :], D_MODEL) * g)


def _outproj(x, w, y, mod, g, rows_per_mod, tm=512, chunk=32):
    rows, d = y.shape
    k = x.shape[1]
    tiles_per_mod = rows_per_mod // tm
    return pl.pallas_call(
        functools.partial(_outproj_kernel, tm=tm, chunk=chunk),
        grid=(rows // tm,),
        in_specs=[
            pl.BlockSpec((tm, k), lambda i: (i, 0)),
            pl.BlockSpec((k, d), lambda i: (0, 0)),
            pl.BlockSpec((tm, d), lambda i: (i, 0)),
            pl.BlockSpec((1, 1, 3 * d), lambda i: (i // tiles_per_mod, 0, 0)),
            pl.BlockSpec((1, d), lambda i: (0, 0)),
        ],
        out_specs=pl.BlockSpec((tm, d), lambda i: (i, 0)),
        out_shape=jax.ShapeDtypeStruct((rows, d), F32),
        compiler_params=_params(("arbitrary",)),
        name="attn_outproj",
    )(x, w, y, mod, g)


PERM_ROWS = 32


def _inproj_perm_kernel(y_ref, mod_ref, g_ref, w_ref, o_ref, slab_scr, h_scr, *, n_batch, tt, n_mod):
    @pl.when(pl.program_id(1) == 0)
    def _():
        g = g_ref[...]
        for b in range(n_batch):
            mb = mod_ref[b if n_mod > 1 else 0]
            shift = mb[:, 0:D_MODEL]
            scale1 = 1.0 + mb[:, D_MODEL:2 * D_MODEL]
            for r0 in range(0, tt, PERM_ROWS):
                h = _norm_mod(y_ref[b, r0:r0 + PERM_ROWS, :], g, scale1, shift)
                for c in range(N_SLABS):
                    slab_scr[c, pl.ds(r0 * n_batch + b, PERM_ROWS, stride=n_batch), :] = (
                        h[:, c * LANES:(c + 1) * LANES])
        for c in range(N_SLABS):
            h_scr[:, c * LANES:(c + 1) * LANES] = slab_scr[c].astype(BF16)

    o_ref[...] = jnp.dot(h_scr[...], w_ref[...], preferred_element_type=F32)


def _inproj_perm(y3, mod, g, w, tm=1024, tn=1024):
    n_batch, t_len, d = y3.shape
    n = w.shape[1]
    tt = tm // n_batch
    n_mod = mod.shape[0]
    return pl.pallas_call(
        functools.partial(_inproj_perm_kernel, n_batch=n_batch, tt=tt, n_mod=n_mod),
        grid=(t_len // tt, n // tn),
        in_specs=[
            pl.BlockSpec((n_batch, tt, d), lambda i, j: (0, i, 0)),
            pl.BlockSpec((n_mod, 1, 3 * d), lambda i, j: (0, 0, 0)),
            pl.BlockSpec((1, d), lambda i, j: (0, 0)),
            pl.BlockSpec((d, tn), lambda i, j: (0, j)),
        ],
        out_specs=pl.BlockSpec((tm, tn), lambda i, j: (i, j)),
        out_shape=jax.ShapeDtypeStruct((n_batch * t_len, n), F32),
        scratch_shapes=[pltpu.VMEM((N_SLABS, tm, LANES), F32), pltpu.VMEM((tm, d), BF16)],
        compiler_params=_params(("arbitrary", "arbitrary")),
        name="lru_inproj",
    )(y3, mod, g, w)


SCAN_STEPS = 8


def _log_sigmoid(x):
    return -(jnp.maximum(-x, 0.0) + jnp.log1p(jnp.exp(-jnp.abs(x))))


def _lru_core_kernel(*refs, n_batch, t_len, chunk, with_final):
    (xb_ref, gb_ref, cw_ref, cb_ref, wa_ref, ba_ref, wx_ref, bx_ref, lam_ref, h0_ref) = refs[:10]
    if with_final:
        y_ref, fin_ref, a_scr, b_scr, xc_scr, hb_scr, w16_scr = refs[10:]
    else:
        y_ref, a_scr, b_scr, xc_scr, hb_scr, w16_scr = refs[10:]
    rows = n_batch * t_len
    c = LRU_BLOCK_DIM
    pad_lo = CONV_LEFT * n_batch
    pad_hi = CONV_RIGHT * n_batch

    a_scr[1, 0:pad_lo, :] = jnp.zeros((pad_lo, c), F32)
    a_scr[1, pad_lo + rows:pad_lo + rows + pad_hi, :] = jnp.zeros((pad_hi, c), F32)
    for r0 in range(0, rows, chunk):
        a_scr[1, pad_lo + r0:pad_lo + r0 + chunk, :] = xb_ref[r0:r0 + chunk, :]
    for r0 in range(0, rows, chunk):
        acc = cb_ref[...]
        for j in range(CONV_WIDTH):
            acc = acc + a_scr[1, r0 + j * n_batch:r0 + j * n_batch + chunk, :] * cw_ref[j:j + 1, :]
        xc_scr[r0:r0 + chunk, :] = acc

    for d in range(2):
        w16_scr[2 * d] = (0.5 * wa_ref[d, 0]).astype(BF16)
        w16_scr[2 * d + 1] = (0.5 * wx_ref[d, 0]).astype(BF16)
    half_ba = [0.5 * ba_ref[d:d + 1, :] for d in range(2)]
    half_bx = [0.5 * bx_ref[d:d + 1, :] for d in range(2)]
    neg_log_s = [-(RG_C * _log_sigmoid(lam_ref[d:d + 1, :])) for d in range(2)]
    exp2_s = [-LOG2_E * neg_log_s[d] for d in range(2)]

    def coeffs(r0):
        x = xc_scr[pl.ds(r0, chunk), :]
        x16 = x.astype(BF16)
        for d in range(2):
            r = 0.5 * jnp.tanh(jnp.dot(x16, w16_scr[2 * d], preferred_element_type=F32) + half_ba[d]) + 0.5
            i = 0.5 * jnp.tanh(jnp.dot(x16, w16_scr[2 * d + 1], preferred_element_type=F32) + half_bx[d]) + 0.5
            a = jnp.exp2(r * exp2_s[d])
            m2 = jnp.tanh(r * neg_log_s[d]) * (a * a + 1.0)
            mult = jnp.where(m2 > 0.0, m2 * lax.rsqrt(m2), 0.0)
            a_scr[d, pl.ds(r0, chunk), :] = a
            b_scr[d, pl.ds(r0, chunk), :] = (mult * i) * x
    _row_loop(rows, chunk, coeffs, unroll=2)

    hf_scr = xc_scr
    if n_batch == 16:
        blk = SCAN_STEPS * 16

        def steps(kb, carry):
            hf, hb = carry
            base_f = pl.multiple_of(kb * blk, blk)
            base_b = pl.multiple_of(rows - blk - kb * blk, blk)
            af = a_scr[0, pl.ds(base_f, blk), :]
            bf = b_scr[0, pl.ds(base_f, blk), :]
            ab = a_scr[1, pl.ds(base_b, blk), :]
            bb = b_scr[1, pl.ds(base_b, blk), :]
            hfs, hbs = [], []
            for j in range(SCAN_STEPS):
                rf = slice(16 * j, 16 * (j + 1))
                rb = slice(blk - 16 * (j + 1), blk - 16 * j)
                hf = af[rf] * hf + bf[rf]
                hb = ab[rb] * hb + bb[rb]
                hfs.append(hf)
                hbs.append(hb)
            hf_scr[pl.ds(base_f, blk), :] = jnp.concatenate(hfs, axis=0)
            hb_scr[pl.ds(base_b, blk), :] = jnp.concatenate(hbs[::-1], axis=0)
            return hf, hb
        hf, hb = lax.fori_loop(0, t_len // SCAN_STEPS, steps, (h0_ref[0], h0_ref[1]))
        if with_final:
            fin_ref[0] = hf
            fin_ref[1] = hb
    else:
        lower = lax.broadcasted_iota(jnp.int32, (8, c), 0) < 4

        blk = SCAN_STEPS * 8

        def steps(kb, carry):
            cf, cb = carry
            base_f = pl.multiple_of(kb * blk, blk)
            base_b = pl.multiple_of(rows - blk - kb * blk, blk)
            for j in range(SCAN_STEPS):
                rf = pl.ds(base_f + 8 * j, 8)
                rb = pl.ds(base_b + 8 * (SCAN_STEPS - 1 - j), 8)
                af = a_scr[0, rf, :]
                bf = b_scr[0, rf, :]
                ab = a_scr[1, rb, :]
                bb = b_scr[1, rb, :]
                uf = af * cf + bf
                ub = ab * cb + bb
                cf = pltpu.roll(af, 4, 0) * uf + pltpu.roll(bf, 4, 0)
                cb = pltpu.roll(ab, 4, 0) * ub + pltpu.roll(bb, 4, 0)
                hf_scr[rf, :] = jnp.where(lower, uf, pltpu.roll(cf, 4, 0))
                hb_scr[rb, :] = jnp.where(lower, pltpu.roll(cb, 4, 0), ub)
            return cf, cb
        lax.fori_loop(0, rows // blk, steps, (h0_ref[0], h0_ref[1]))

    def emit(r0):
        h = hf_scr[pl.ds(r0, chunk), :] + hb_scr[pl.ds(r0, chunk), :]
        y_ref[pl.ds(r0, chunk), :] = (h * _silu(gb_ref[pl.ds(r0, chunk), :])).astype(BF16)
    _row_loop(rows, chunk, emit)


def _lru_core(xg, n_batch, t_len, conv_w, conv_b, w_a, b_a, w_x, b_x, lam, h0, with_final, chunk=256):
    rows = n_batch * t_len
    c = LRU_BLOCK_DIM
    h_rows = h0.shape[1]
    pad = (CONV_WIDTH - 1) * n_batch
    in_specs = [
        pl.BlockSpec((rows, c), lambda i: (0, i)),
        pl.BlockSpec((rows, c), lambda i: (0, LRU_BLOCKS + i)),
        pl.BlockSpec((CONV_WIDTH, c), lambda i: (0, i)),
        pl.BlockSpec((1, c), lambda i: (0, i)),
        pl.BlockSpec((2, 1, c, c), lambda i: (0, i, 0, 0)),
        pl.BlockSpec((2, c), lambda i: (0, i)),
        pl.BlockSpec((2, 1, c, c), lambda i: (0, i, 0, 0)),
        pl.BlockSpec((2, c), lambda i: (0, i)),
        pl.BlockSpec((2, c), lambda i: (0, i)),
        pl.BlockSpec((2, h_rows, c), lambda i: (0, 0, i)),
    ]
    y_spec = pl.BlockSpec((rows, c), lambda i: (0, i))
    y_shape = jax.ShapeDtypeStruct((rows, LRU_WIDTH), BF16)
    if with_final:
        out_specs = (y_spec, pl.BlockSpec((2, n_batch, c), lambda i: (0, 0, i)))
        out_shape = (y_shape, jax.ShapeDtypeStruct((2, n_batch, LRU_WIDTH), F32))
    else:
        out_specs = y_spec
        out_shape = y_shape
    return pl.pallas_call(
        functools.partial(_lru_core_kernel, n_batch=n_batch, t_len=t_len, chunk=chunk, with_final=with_final),
        grid=(LRU_BLOCKS,),
        in_specs=in_specs,
        out_specs=out_specs,
        out_shape=out_shape,
        scratch_shapes=[
            pltpu.VMEM((2, rows + pad, c), F32),
            pltpu.VMEM((2, rows, c), F32),
            pltpu.VMEM((rows, c), F32),
            pltpu.VMEM((rows, c), F32),
            pltpu.VMEM((4, c, c), BF16),
        ],
        compiler_params=_params(("arbitrary",)),
        name="lru_core",
    )(xg, xg, conv_w, conv_b, w_a, b_a, w_x, b_x, lam, h0)


MXU_COLS = 256


def _outproj_perm_kernel(x_ref, w_ref, y_ref, mod_ref, g_ref, o_ref, *slab_scrs, n_batch, tt, n_mod):
    g = g_ref[...]
    sub_t = OUT_SUB_ROWS // n_batch
    perm = min(PERM_ROWS, sub_t)
    def project(blk):
        rows = slice(blk * OUT_SUB_ROWS, (blk + 1) * OUT_SUB_ROWS)
        for n0 in range(0, D_MODEL, MXU_COLS):
            m = jnp.dot(x_ref[rows, :], w_ref[:, n0:n0 + MXU_COLS], preferred_element_type=F32)
            for c in range(MXU_COLS // LANES):
                slab_scrs[blk][n0 // LANES + c] = m[:, c * LANES:(c + 1) * LANES]

    project(0)
    for blk, slab_scr in enumerate(slab_scrs):
        if blk + 1 < len(slab_scrs):
            project(blk + 1)
        t0 = blk * sub_t
        for b in range(n_batch):
            gate = mod_ref[b if n_mod > 1 else 0][:, 2 * D_MODEL:3 * D_MODEL]
            for r0 in range(0, sub_t, perm):
                m = jnp.concatenate(
                    [slab_scr[c, pl.ds(r0 * n_batch + b, perm, stride=n_batch), :] for c in range(N_SLABS)],
                    axis=-1)
                out_rows = slice(t0 + r0, t0 + r0 + perm)
                o_ref[b, out_rows, :] = y_ref[b, out_rows, :] + gate * (_rms(m, D_MODEL) * g)


def _outproj_perm(x, w, y3, mod, g, tm=512):
    n_batch, t_len, d = y3.shape
    k = x.shape[1]
    tt = tm // n_batch
    n_mod = mod.shape[0]
    return pl.pallas_call(
        functools.partial(_outproj_perm_kernel, n_batch=n_batch, tt=tt, n_mod=n_mod),
        grid=(t_len // tt,),
        in_specs=[
            pl.BlockSpec((tm, k), lambda i: (i, 0)),
            pl.BlockSpec((k, d), lambda i: (0, 0)),
            pl.BlockSpec((n_batch, tt, d), lambda i: (0, i, 0)),
            pl.BlockSpec((n_mod, 1, 3 * d), lambda i: (0, 0, 0)),
            pl.BlockSpec((1, d), lambda i: (0, 0)),
        ],
        out_specs=pl.BlockSpec((n_batch, tt, d), lambda i: (0, i, 0)),
        out_shape=jax.ShapeDtypeStruct((n_batch, t_len, d), F32),
        scratch_shapes=[pltpu.VMEM((N_SLABS, OUT_SUB_ROWS, LANES), F32)] * (tm // OUT_SUB_ROWS),
        compiler_params=_params(("arbitrary",)),
        name="lru_outproj",
    )(x, w, y3, mod, g)


def kernel(x_prompt, x_sample, c, cache_k, cache_v, state_lru, c_ctx, w_mod, b_mod, g_pre, g_post,
           w_in_attn, q_norm, k_norm, w_out_attn, w_in_lru, conv_w, conv_b,
           w_rg_a, b_rg_a, w_rg_x, b_rg_x, lru_lambda, w_out_lru):
    n_p, t_p, d = x_prompt.shape
    n_s, t_s, _ = x_sample.shape
    depth = w_mod.shape[0]
    n_ctx = cache_k.shape[2]

    cond = jnp.concatenate([c_ctx[None, :], c, jnp.zeros((COND_ROWS - 1 - n_s, d), F32)], axis=0)
    mod = _modulation(cond, w_mod, b_mod)

    y_p = x_prompt
    y_s = x_sample
    new_k, new_v, new_h = [], [], []
    for l in range(depth):
        j = l // 2
        mod_p = mod[l, 0:1].reshape(1, 1, 3 * d)
        mod_s = mod[l, 1:1 + n_s].reshape(n_s, 1, 3 * d)
        g_pre_l = g_pre[l].reshape(1, d)
        g_post_l = g_post[l].reshape(1, d)
        if l % 2 == 0:
            w_in = w_in_attn[j].astype(BF16)
            w_out = w_out_attn[j].astype(BF16)
            qn = q_norm[j].reshape(1, HEAD_DIM)
            kn = k_norm[j].reshape(1, HEAD_DIM)
            yp2 = y_p.reshape(n_p * t_p, d)
            ys2 = y_s.reshape(n_s * t_s, d)
            pp = _inproj(yp2, mod_p, g_pre_l, w_in, rows_per_mod=n_p * t_p)
            ps = _inproj(ys2, mod_s, g_pre_l, w_in, rows_per_mod=t_s)
            og_p, k_p, v_p = _attention(pp, n_p, t_p, qn, kn, n_seq=4)
            ck = cache_k[:, j].reshape(n_s * n_ctx, KV_WIDTH)
            cv = cache_v[:, j].reshape(n_s * n_ctx, KV_WIDTH)
            og_s = _attention(ps, n_s, t_s, qn, kn, cache=(ck, cv), rope=_rope_tables(t_s))
            y_p = _outproj(og_p, w_out, yp2, mod_p, g_post_l, rows_per_mod=n_p * t_p).reshape(n_p, t_p, d)
            y_s = _outproj(og_s, w_out, ys2, mod_s, g_post_l, rows_per_mod=t_s).reshape(n_s, t_s, d)
            new_k.append(k_p.reshape(n_p, t_p, N_KV_HEADS, HEAD_DIM))
            new_v.append(v_p.reshape(n_p, t_p, N_KV_HEADS, HEAD_DIM))
        else:
            w_in = w_in_lru[j].astype(BF16)
            w_out = w_out_lru[j].astype(BF16)
            cb = conv_b[j].reshape(1, LRU_WIDTH)
            lru_args = (conv_w[j], cb, w_rg_a[j], b_rg_a[j], w_rg_x[j], b_rg_x[j], lru_lambda[j])
            xg_p = _inproj_perm(y_p, mod_p, g_pre_l, w_in)
            xg_s = _inproj_perm(y_s, mod_s, g_pre_l, w_in)
            h0_p = jnp.zeros((2, n_p, LRU_WIDTH), F32)
            h0_s = jnp.swapaxes(state_lru[:, j], 0, 1)
            h0_s = jnp.concatenate([h0_s, h0_s], axis=1)
            yg_p, fin = _lru_core(xg_p, n_p, t_p, *lru_args, h0_p, with_final=True)
            yg_s = _lru_core(xg_s, n_s, t_s, *lru_args, h0_s, with_final=False)
            y_p = _outproj_perm(yg_p, w_out, y_p, mod_p, g_post_l)
            y_s = _outproj_perm(yg_s, w_out, y_s, mod_s, g_post_l)
            new_h.append(jnp.swapaxes(fin, 0, 1))
    new_cache_k = jnp.stack(new_k, axis=1)
    new_cache_v = jnp.stack(new_v, axis=1)
    new_state_lru = jnp.stack(new_h, axis=1)
    return (y_p, y_s, new_cache_k, new_cache_v, new_state_lru)
```

```python
import functools

import jax
import jax.numpy as jnp
from jax import lax
from jax.experimental import pallas as pl
from jax.experimental.pallas import tpu as pltpu

F32 = jnp.float32
BF16 = jnp.bfloat16

D_MODEL = 2048
HEAD_DIM = 128
N_HEADS = 16
N_KV_HEADS = 4
Q_PER_KV = N_HEADS // N_KV_HEADS
ATTN_WIDTH = N_HEADS * HEAD_DIM
KV_WIDTH = N_KV_HEADS * HEAD_DIM
GRID_W = 64
AXIS_DIM = HEAD_DIM // 2
ROPE_THETA = 10000.0
LRU_WIDTH = D_MODEL
LRU_BLOCKS = 8
LRU_BLOCK_DIM = LRU_WIDTH // LRU_BLOCKS
CONV_WIDTH = 4
CONV_LEFT = (CONV_WIDTH - 1) // 2
CONV_RIGHT = CONV_WIDTH - 1 - CONV_LEFT
RG_C = 8.0
EPS = 1e-6
LOG2_E = 1.4426950408889634

LANES = 128
N_SLABS = D_MODEL // LANES
COND_ROWS = 8
VMEM_LIMIT = 60 * 1024 * 1024


def _params(sem):
    return pltpu.CompilerParams(dimension_semantics=sem, vmem_limit_bytes=VMEM_LIMIT)


def _row_loop(n_rows, chunk, body, unroll=1):
    def step(i, carry):
        body(pl.multiple_of(i * chunk, chunk))
        return carry
    lax.fori_loop(0, n_rows // chunk, step, 0, unroll=unroll)


def _rms(x, width):
    return x * lax.rsqrt(jnp.sum(x * x, axis=-1, keepdims=True) * (1.0 / width) + EPS)


def _sigmoid(x):
    return 0.5 * jnp.tanh(0.5 * x) + 0.5


def _silu(x):
    return x * _sigmoid(x)


def _mod_kernel(cond_ref, w_ref, b_ref, o_ref):
    x = _silu(cond_ref[...]).astype(BF16)
    o_ref[0] = jnp.dot(x, w_ref[0].astype(BF16), preferred_element_type=F32) + b_ref[0]


def _modulation(cond, w_mod, b_mod, tn=1024):
    depth, d, n = w_mod.shape
    return pl.pallas_call(
        _mod_kernel,
        grid=(depth, n // tn),
        in_specs=[
            pl.BlockSpec((COND_ROWS, d), lambda l, j: (0, 0)),
            pl.BlockSpec((1, d, tn), lambda l, j: (l, 0, j)),
            pl.BlockSpec((1, 1, tn), lambda l, j: (l, 0, j)),
        ],
        out_specs=pl.BlockSpec((1, COND_ROWS, tn), lambda l, j: (l, 0, j)),
        out_shape=jax.ShapeDtypeStruct((depth, COND_ROWS, n), F32),
        compiler_params=_params(("arbitrary", "arbitrary")),
        name="modulation",
    )(cond, w_mod, b_mod.reshape(depth, 1, n))


def _norm_mod(x, g, scale1, shift):
    return (_rms(x, D_MODEL) * g) * scale1 + shift


def _inproj_kernel(y_ref, mod_ref, g_ref, w_ref, o_ref, h_scr, *, tm, chunk):
    @pl.when(pl.program_id(1) == 0)
    def _():
        g = g_ref[...]
        shift = mod_ref[0, :, 0:D_MODEL]
        scale1 = 1.0 + mod_ref[0, :, D_MODEL:2 * D_MODEL]

        def body(r0):
            x = y_ref[pl.ds(r0, chunk), :]
            h_scr[pl.ds(r0, chunk), :] = _norm_mod(x, g, scale1, shift).astype(BF16)
        _row_loop(tm, chunk, body, unroll=4)

    o_ref[...] = jnp.dot(h_scr[...], w_ref[...], preferred_element_type=F32)


def _inproj(y, mod, g, w, rows_per_mod, tm=1024, tn=1024, chunk=32):
    rows, d = y.shape
    n = w.shape[1]
    tiles_per_mod = rows_per_mod // tm
    return pl.pallas_call(
        functools.partial(_inproj_kernel, tm=tm, chunk=chunk),
        grid=(rows // tm, n // tn),
        in_specs=[
            pl.BlockSpec((tm, d), lambda i, j: (i, 0)),
            pl.BlockSpec((1, 1, 3 * d), lambda i, j: (i // tiles_per_mod, 0, 0)),
            pl.BlockSpec((1, d), lambda i, j: (0, 0)),
            pl.BlockSpec((d, tn), lambda i, j: (0, j)),
        ],
        out_specs=pl.BlockSpec((tm, tn), lambda i, j: (i, j)),
        out_shape=jax.ShapeDtypeStruct((rows, n), F32),
        scratch_shapes=[pltpu.VMEM((tm, d), BF16)],
        compiler_params=_params(("arbitrary", "arbitrary")),
        name="attn_inproj",
    )(y, mod, g, w)


ATTN_SUB_ROWS = 256


def _rope(x, cos, sin_lo, sin_hi):
    return x * cos + pltpu.roll(x, HEAD_DIM - AXIS_DIM // 2, 1) * sin_lo + pltpu.roll(x, AXIS_DIM // 2, 1) * sin_hi


def _attn_kernel(*refs, t_len, tq, n_ctx, n_seq, latent):
    if latent:
        (q_ref, k_ref, v_ref, g_ref, ck_ref, cv_ref, qn_ref, kn_ref,
         cosq_ref, sloq_ref, shiq_ref, cosk_ref, slok_ref, shik_ref,
         og_ref, k_scr, v_scr) = refs
    else:
        (q_ref, k_ref, v_ref, g_ref, qn_ref, kn_ref,
         og_ref, ko_ref, vo_ref, k_scr, v_scr) = refs

    @pl.when(pl.program_id(2) == 0)
    def _():
        for s in range(n_seq):
            rows = slice(s * t_len, (s + 1) * t_len)
            kn = _rms(k_ref[rows, :], HEAD_DIM) * kn_ref[...]
            v = v_ref[rows, :]
            if latent:
                kn = _rope(kn, cosk_ref[...], slok_ref[...], shik_ref[...])
                k_scr[s, 0:n_ctx, :] = ck_ref[...].astype(BF16)
                v_scr[s, 0:n_ctx, :] = cv_ref[...].astype(BF16)
            else:
                ko_ref[rows, :] = kn
                vo_ref[rows, :] = v
            k_scr[s, n_ctx:n_ctx + t_len, :] = kn.astype(BF16)
            v_scr[s, n_ctx:n_ctx + t_len, :] = v.astype(BF16)

    qn = qn_ref[...]
    q_scale = (HEAD_DIM ** -0.5) * LOG2_E
    chains = [(s, s * tq + r0, slice(h * HEAD_DIM, (h + 1) * HEAD_DIM))
              for s in range(n_seq) for r0 in range(0, tq, ATTN_SUB_ROWS) for h in range(Q_PER_KV)]

    def scores(s, r0, cols):
        rows = slice(r0, r0 + ATTN_SUB_ROWS)
        qh = _rms(q_ref[rows, cols], HEAD_DIM) * qn
        if latent:
            qh = _rope(qh, cosq_ref[rows, :], sloq_ref[rows, :], shiq_ref[rows, :])
        qh = (qh * q_scale).astype(BF16)
        return lax.dot_general(qh, k_scr[s], (((1,), (1,)), ((), ())), preferred_element_type=F32)

    sc_next = scores(*chains[0])
    for i, (s, r0, cols) in enumerate(chains):
        rows = slice(r0, r0 + ATTN_SUB_ROWS)
        sc = sc_next
        if i + 1 < len(chains):
            sc_next = scores(*chains[i + 1])
        e = jnp.exp2(sc - jnp.max(sc, axis=-1, keepdims=True))
        denom = jnp.sum(e, axis=-1, keepdims=True)
        o = jnp.dot(e.astype(BF16), v_scr[s], preferred_element_type=F32) / denom
        og_ref[rows, cols] = (o * _silu(g_ref[rows, cols])).astype(BF16)


def _attention(p, n_batch, t_len, q_norm, k_norm, cache=None, rope=None, tq=512, n_seq=1):
    tq = min(tq, t_len)
    assert n_seq == 1 or tq == t_len
    rows = n_batch * t_len
    nq = t_len // tq
    gw = Q_PER_KV * HEAD_DIM
    k_col = ATTN_WIDTH // HEAD_DIM
    v_col = (ATTN_WIDTH + KV_WIDTH) // HEAD_DIM
    g_col = (ATTN_WIDTH + 2 * KV_WIDTH) // gw
    latent = cache is not None
    n_ctx = cache[0].shape[0] // n_batch if latent else 0

    in_specs = [
        pl.BlockSpec((n_seq * tq, gw), lambda b, kv, qi: (b * nq + qi, kv)),
        pl.BlockSpec((n_seq * t_len, HEAD_DIM), lambda b, kv, qi: (b, k_col + kv)),
        pl.BlockSpec((n_seq * t_len, HEAD_DIM), lambda b, kv, qi: (b, v_col + kv)),
        pl.BlockSpec((n_seq * tq, gw), lambda b, kv, qi: (b * nq + qi, g_col + kv)),
    ]
    args = [p, p, p, p]
    if latent:
        in_specs += [pl.BlockSpec((n_ctx, HEAD_DIM), lambda b, kv, qi: (b, kv))] * 2
        args += list(cache)
    in_specs += [pl.BlockSpec((1, HEAD_DIM), lambda b, kv, qi: (0, 0))] * 2
    args += [q_norm, k_norm]
    og_spec = pl.BlockSpec((n_seq * tq, gw), lambda b, kv, qi: (b * nq + qi, kv))
    og_shape = jax.ShapeDtypeStruct((rows, ATTN_WIDTH), BF16)
    if latent:
        in_specs += [pl.BlockSpec((tq, HEAD_DIM), lambda b, kv, qi: (qi, 0))] * 3
        in_specs += [pl.BlockSpec((t_len, HEAD_DIM), lambda b, kv, qi: (0, 0))] * 3
        args += list(rope) + list(rope)
        out_specs = og_spec
        out_shape = og_shape
    else:
        kv_spec = pl.BlockSpec((n_seq * t_len, HEAD_DIM), lambda b, kv, qi: (b, kv))
        kv_shape = jax.ShapeDtypeStruct((rows, KV_WIDTH), F32)
        out_specs = (og_spec, kv_spec, kv_spec)
        out_shape = (og_shape, kv_shape, kv_shape)
    return pl.pallas_call(
        functools.partial(_attn_kernel, t_len=t_len, tq=tq, n_ctx=n_ctx, n_seq=n_seq, latent=latent),
        grid=(n_batch // n_seq, N_KV_HEADS, nq),
        in_specs=in_specs,
        out_specs=out_specs,
        out_shape=out_shape,
        scratch_shapes=[pltpu.VMEM((n_seq, n_ctx + t_len, HEAD_DIM), BF16)] * 2,
        compiler_params=_params(("arbitrary", "arbitrary", "arbitrary")),
        name="attn_latent" if latent else "attn_context",
    )(*args)


def _rope_tables(n):
    rows = n // GRID_W
    row = jnp.repeat(jnp.arange(rows, dtype=F32), GRID_W)
    col = jnp.tile(jnp.arange(GRID_W, dtype=F32), rows)
    inv = ROPE_THETA ** (-jnp.arange(0, AXIS_DIM, 2, dtype=F32) / AXIS_DIM)
    ar = row[:, None] * inv
    ac = col[:, None] * inv
    ang = jnp.concatenate([ar, ar, ac, ac], axis=-1)
    cos, sin = jnp.cos(ang), jnp.sin(ang)
    first = (jnp.arange(HEAD_DIM) % AXIS_DIM) < AXIS_DIM // 2
    return cos, jnp.where(first, -sin, 0.0), jnp.where(first, 0.0, sin)


OUT_SUB_ROWS = 256


def _outproj_kernel(x_ref, w_ref, y_ref, mod_ref, g_ref, o_ref, *, tm, chunk):
    g = g_ref[...]
    gate = mod_ref[0, :, 2 * D_MODEL:3 * D_MODEL]
    def project(r0):
        return jnp.dot(x_ref[r0:r0 + OUT_SUB_ROWS, :], w_ref[...], preferred_element_type=F32)

    m_next = project(0)
    for r0 in range(0, tm, OUT_SUB_ROWS):
        m = m_next
        if r0 + OUT_SUB_ROWS < tm:
            m_next = project(r0 + OUT_SUB_ROWS)
        for c0 in range(0, OUT_SUB_ROWS, chunk):
            rows = slice(r0 + c0, r0 + c0 + chunk)
            o_ref[rows, :] = y_ref[rows, :] + gate * (_rms(m[c0:c0 + chunk, :], D_MODEL) * g)


def _outproj_pair_kernel(xa_ref, xb_ref, w_ref, ya_ref, yb_ref, moda_ref, modb_ref, g_ref, oa_ref, ob_ref,
                         *, tiles_a, tm, chunk):
    @pl.when(pl.program_id(0) < tiles_a)
    def _():
        _outproj_kernel(xa_ref, w_ref, ya_ref, moda_ref, g_ref, oa_ref, tm=tm, chunk=chunk)

    @pl.when(pl.program_id(0) >= tiles_a)
    def _():
        _outproj_kernel(xb_ref, w_ref, yb_ref, modb_ref, g_ref, ob_ref, tm=tm, chunk=chunk)


def _outproj_pair(xa, xb, w, ya, yb, moda, modb, g, rows_per_mod_b, tm=512, chunk=32):
    d = ya.shape[1]
    k = xa.shape[1]
    tiles_a = ya.shape[0] // tm
    tiles_b = yb.shape[0] // tm
    tiles_per_mod = rows_per_mod_b // tm
    ia = lambda i: (jnp.minimum(i, tiles_a - 1), 0)
    ib = lambda i: (jnp.maximum(i - tiles_a, 0), 0)
    return pl.pallas_call(
        functools.partial(_outproj_pair_kernel, tiles_a=tiles_a, tm=tm, chunk=chunk),
        grid=(tiles_a + tiles_b,),
        in_specs=[
            pl.BlockSpec((tm, k), ia),
            pl.BlockSpec((tm, k), ib),
            pl.BlockSpec((k, d), lambda i: (0, 0), pipeline_mode=pl.Buffered(1)),
            pl.BlockSpec((tm, d), ia),
            pl.BlockSpec((tm, d), ib),
            pl.BlockSpec((1, 1, 3 * d), lambda i: (0, 0, 0)),
            pl.BlockSpec((1, 1, 3 * d), lambda i: (jnp.maximum(i - tiles_a, 0) // tiles_per_mod, 0, 0)),
            pl.BlockSpec((1, d), lambda i: (0, 0)),
        ],
        out_specs=(pl.BlockSpec((tm, d), ia), pl.BlockSpec((tm, d), ib)),
        out_shape=(jax.ShapeDtypeStruct(ya.shape, F32), jax.ShapeDtypeStruct(yb.shape, F32)),
        compiler_params=_params(("arbitrary",)),
        name="attn_outproj",
    )(xa, xb, w, ya, yb, moda, modb, g)


PERM_ROWS = 32


def _inproj_perm_kernel(y_ref, mod_ref, g_ref, w_ref, o_ref, slab_scr, h_scr, *, n_batch, tt, n_mod):
    @pl.when(pl.program_id(1) == 0)
    def _():
        g = g_ref[...]
        for b in range(n_batch):
            mb = mod_ref[b if n_mod > 1 else 0]
            shift = mb[:, 0:D_MODEL]
            scale1 = 1.0 + mb[:, D_MODEL:2 * D_MODEL]
            for r0 in range(0, tt, PERM_ROWS):
                h = _norm_mod(y_ref[b, r0:r0 + PERM_ROWS, :], g, scale1, shift)
                for c in range(N_SLABS):
                    slab_scr[c, pl.ds(r0 * n_batch + b, PERM_ROWS, stride=n_batch), :] = (
                        h[:, c * LANES:(c + 1) * LANES])
        for c in range(N_SLABS):
            h_scr[:, c * LANES:(c + 1) * LANES] = slab_scr[c].astype(BF16)

    o_ref[...] = jnp.dot(h_scr[...], w_ref[...], preferred_element_type=F32)


def _inproj_perm(y3, mod, g, w, tm=1024, tn=1024):
    n_batch, t_len, d = y3.shape
    n = w.shape[1]
    tt = tm // n_batch
    n_mod = mod.shape[0]
    return pl.pallas_call(
        functools.partial(_inproj_perm_kernel, n_batch=n_batch, tt=tt, n_mod=n_mod),
        grid=(t_len // tt, n // tn),
        in_specs=[
            pl.BlockSpec((n_batch, tt, d), lambda i, j: (0, i, 0)),
            pl.BlockSpec((n_mod, 1, 3 * d), lambda i, j: (0, 0, 0)),
            pl.BlockSpec((1, d), lambda i, j: (0, 0)),
            pl.BlockSpec((d, tn), lambda i, j: (0, j)),
        ],
        out_specs=pl.BlockSpec((tm, tn), lambda i, j: (i, j)),
        out_shape=jax.ShapeDtypeStruct((n_batch * t_len, n), F32),
        scratch_shapes=[pltpu.VMEM((N_SLABS, tm, LANES), F32), pltpu.VMEM((tm, d), BF16)],
        compiler_params=_params(("arbitrary", "arbitrary")),
        name="lru_inproj",
    )(y3, mod, g, w)


SCAN_STEPS = 8


def _log_sigmoid(x):
    return -(jnp.maximum(-x, 0.0) + jnp.log1p(jnp.exp(-jnp.abs(x))))


def _lru_core_kernel(*refs, n_batch, t_len, chunk, with_final):
    (xb_ref, gb_ref, cw_ref, cb_ref, wa_ref, ba_ref, wx_ref, bx_ref, lam_ref, h0_ref) = refs[:10]
    if with_final:
        y_ref, fin_ref, a_scr, b_scr, xc_scr, hb_scr, w16_scr = refs[10:]
    else:
        y_ref, a_scr, b_scr, xc_scr, hb_scr, w16_scr = refs[10:]
    rows = n_batch * t_len
    c = LRU_BLOCK_DIM
    pad_lo = CONV_LEFT * n_batch
    pad_hi = CONV_RIGHT * n_batch

    a_scr[1, 0:pad_lo, :] = jnp.zeros((pad_lo, c), F32)
    a_scr[1, pad_lo + rows:pad_lo + rows + pad_hi, :] = jnp.zeros((pad_hi, c), F32)
    for r0 in range(0, rows, chunk):
        a_scr[1, pad_lo + r0:pad_lo + r0 + chunk, :] = xb_ref[r0:r0 + chunk, :]
    for r0 in range(0, rows, chunk):
        acc = cb_ref[...]
        for j in range(CONV_WIDTH):
            acc = acc + a_scr[1, r0 + j * n_batch:r0 + j * n_batch + chunk, :] * cw_ref[j:j + 1, :]
        xc_scr[r0:r0 + chunk, :] = acc

    for d in range(2):
        w16_scr[2 * d] = (0.5 * wa_ref[d, 0]).astype(BF16)
        w16_scr[2 * d + 1] = (0.5 * wx_ref[d, 0]).astype(BF16)
    half_ba = [0.5 * ba_ref[d:d + 1, :] for d in range(2)]
    half_bx = [0.5 * bx_ref[d:d + 1, :] for d in range(2)]
    neg_log_s = [-(RG_C * _log_sigmoid(lam_ref[d:d + 1, :])) for d in range(2)]
    exp2_s = [-LOG2_E * neg_log_s[d] for d in range(2)]

    def coeffs(r0):
        x = xc_scr[pl.ds(r0, chunk), :]
        x16 = x.astype(BF16)
        for d in range(2):
            r = 0.5 * jnp.tanh(jnp.dot(x16, w16_scr[2 * d], preferred_element_type=F32) + half_ba[d]) + 0.5
            i = 0.5 * jnp.tanh(jnp.dot(x16, w16_scr[2 * d + 1], preferred_element_type=F32) + half_bx[d]) + 0.5
            a = jnp.exp2(r * exp2_s[d])
            m2 = jnp.tanh(r * neg_log_s[d]) * (a * a + 1.0)
            mult = jnp.where(m2 > 0.0, m2 * lax.rsqrt(m2), 0.0)
            a_scr[d, pl.ds(r0, chunk), :] = a
            b_scr[d, pl.ds(r0, chunk), :] = (mult * i) * x
    _row_loop(rows, chunk, coeffs, unroll=2)

    hf_scr = xc_scr
    if n_batch == 16:
        blk = SCAN_STEPS * 16

        def steps(kb, carry):
            hf, hb = carry
            base_f = pl.multiple_of(kb * blk, blk)
            base_b = pl.multiple_of(rows - blk - kb * blk, blk)
            af = a_scr[0, pl.ds(base_f, blk), :]
            bf = b_scr[0, pl.ds(base_f, blk), :]
            ab = a_scr[1, pl.ds(base_b, blk), :]
            bb = b_scr[1, pl.ds(base_b, blk), :]
            hfs, hbs = [], []
            for j in range(SCAN_STEPS):
                rf = slice(16 * j, 16 * (j + 1))
                rb = slice(blk - 16 * (j + 1), blk - 16 * j)
                hf = af[rf] * hf + bf[rf]
                hb = ab[rb] * hb + bb[rb]
                hfs.append(hf)
                hbs.append(hb)
            hf_scr[pl.ds(base_f, blk), :] = jnp.concatenate(hfs, axis=0)
            hb_scr[pl.ds(base_b, blk), :] = jnp.concatenate(hbs[::-1], axis=0)
            return hf, hb
        hf, hb = lax.fori_loop(0, t_len // SCAN_STEPS, steps, (h0_ref[0], h0_ref[1]))
        if with_final:
            fin_ref[0] = hf
            fin_ref[1] = hb
    else:
        lower = lax.broadcasted_iota(jnp.int32, (8, c), 0) < 4

        blk = SCAN_STEPS * 8

        def steps(kb, carry):
            cf, cb = carry
            base_f = pl.multiple_of(kb * blk, blk)
            base_b = pl.multiple_of(rows - blk - kb * blk, blk)
            for j in range(SCAN_STEPS):
                rf = pl.ds(base_f + 8 * j, 8)
                rb = pl.ds(base_b + 8 * (SCAN_STEPS - 1 - j), 8)
                af = a_scr[0, rf, :]
                bf = b_scr[0, rf, :]
                ab = a_scr[1, rb, :]
                bb = b_scr[1, rb, :]
                uf = af * cf + bf
                ub = ab * cb + bb
                cf = pltpu.roll(af, 4, 0) * uf + pltpu.roll(bf, 4, 0)
                cb = pltpu.roll(ab, 4, 0) * ub + pltpu.roll(bb, 4, 0)
                hf_scr[rf, :] = jnp.where(lower, uf, pltpu.roll(cf, 4, 0))
                hb_scr[rb, :] = jnp.where(lower, pltpu.roll(cb, 4, 0), ub)
            return cf, cb
        lax.fori_loop(0, rows // blk, steps, (h0_ref[0], h0_ref[1]))

    def emit(r0):
        h = hf_scr[pl.ds(r0, chunk), :] + hb_scr[pl.ds(r0, chunk), :]
        y_ref[pl.ds(r0, chunk), :] = (h * _silu(gb_ref[pl.ds(r0, chunk), :])).astype(BF16)
    _row_loop(rows, chunk, emit)


def _lru_core(xg, n_batch, t_len, conv_w, conv_b, w_a, b_a, w_x, b_x, lam, h0, with_final, chunk=256):
    rows = n_batch * t_len
    c = LRU_BLOCK_DIM
    h_rows = h0.shape[1]
    pad = (CONV_WIDTH - 1) * n_batch
    in_specs = [
        pl.BlockSpec((rows, c), lambda i: (0, i)),
        pl.BlockSpec((rows, c), lambda i: (0, LRU_BLOCKS + i)),
        pl.BlockSpec((CONV_WIDTH, c), lambda i: (0, i)),
        pl.BlockSpec((1, c), lambda i: (0, i)),
        pl.BlockSpec((2, 1, c, c), lambda i: (0, i, 0, 0)),
        pl.BlockSpec((2, c), lambda i: (0, i)),
        pl.BlockSpec((2, 1, c, c), lambda i: (0, i, 0, 0)),
        pl.BlockSpec((2, c), lambda i: (0, i)),
        pl.BlockSpec((2, c), lambda i: (0, i)),
        pl.BlockSpec((2, h_rows, c), lambda i: (0, 0, i)),
    ]
    y_spec = pl.BlockSpec((rows, c), lambda i: (0, i))
    y_shape = jax.ShapeDtypeStruct((rows, LRU_WIDTH), BF16)
    if with_final:
        out_specs = (y_spec, pl.BlockSpec((2, n_batch, c), lambda i: (0, 0, i)))
        out_shape = (y_shape, jax.ShapeDtypeStruct((2, n_batch, LRU_WIDTH), F32))
    else:
        out_specs = y_spec
        out_shape = y_shape
    return pl.pallas_call(
        functools.partial(_lru_core_kernel, n_batch=n_batch, t_len=t_len, chunk=chunk, with_final=with_final),
        grid=(LRU_BLOCKS,),
        in_specs=in_specs,
        out_specs=out_specs,
        out_shape=out_shape,
        scratch_shapes=[
            pltpu.VMEM((2, rows + pad, c), F32),
            pltpu.VMEM((2, rows, c), F32),
            pltpu.VMEM((rows, c), F32),
            pltpu.VMEM((rows, c), F32),
            pltpu.VMEM((4, c, c), BF16),
        ],
        compiler_params=_params(("arbitrary",)),
        name="lru_core",
    )(xg, xg, conv_w, conv_b, w_a, b_a, w_x, b_x, lam, h0)


MXU_COLS = 256


def _outproj_perm_kernel(x_ref, w_ref, y_ref, mod_ref, g_ref, o_ref, *slab_scrs, n_batch, tt, n_mod):
    g = g_ref[...]
    sub_t = OUT_SUB_ROWS // n_batch
    perm = min(PERM_ROWS, sub_t)
    def project(blk):
        rows = slice(blk * OUT_SUB_ROWS, (blk + 1) * OUT_SUB_ROWS)
        for n0 in range(0, D_MODEL, MXU_COLS):
            m = jnp.dot(x_ref[rows, :], w_ref[:, n0:n0 + MXU_COLS], preferred_element_type=F32)
            for c in range(MXU_COLS // LANES):
                slab_scrs[blk][n0 // LANES + c] = m[:, c * LANES:(c + 1) * LANES]

    project(0)
    for blk, slab_scr in enumerate(slab_scrs):
        if blk + 1 < len(slab_scrs):
            project(blk + 1)
        t0 = blk * sub_t
        for b in range(n_batch):
            gate = mod_ref[b if n_mod > 1 else 0][:, 2 * D_MODEL:3 * D_MODEL]
            for r0 in range(0, sub_t, perm):
                m = jnp.concatenate(
                    [slab_scr[c, pl.ds(r0 * n_batch + b, perm, stride=n_batch), :] for c in range(N_SLABS)],
                    axis=-1)
                out_rows = slice(t0 + r0, t0 + r0 + perm)
                o_ref[b, out_rows, :] = y_ref[b, out_rows, :] + gate * (_rms(m, D_MODEL) * g)


def _outproj_perm(x, w, y3, mod, g, tm=512):
    n_batch, t_len, d = y3.shape
    k = x.shape[1]
    tt = tm // n_batch
    n_mod = mod.shape[0]
    return pl.pallas_call(
        functools.partial(_outproj_perm_kernel, n_batch=n_batch, tt=tt, n_mod=n_mod),
        grid=(t_len // tt,),
        in_specs=[
            pl.BlockSpec((tm, k), lambda i: (i, 0)),
            pl.BlockSpec((k, d), lambda i: (0, 0)),
            pl.BlockSpec((n_batch, tt, d), lambda i: (0, i, 0)),
            pl.BlockSpec((n_mod, 1, 3 * d), lambda i: (0, 0, 0)),
            pl.BlockSpec((1, d), lambda i: (0, 0)),
        ],
        out_specs=pl.BlockSpec((n_batch, tt, d), lambda i: (0, i, 0)),
        out_shape=jax.ShapeDtypeStruct((n_batch, t_len, d), F32),
        scratch_shapes=[pltpu.VMEM((N_SLABS, OUT_SUB_ROWS, LANES), F32)] * (tm // OUT_SUB_ROWS),
        compiler_params=_params(("arbitrary",)),
        name="lru_outproj",
    )(x, w, y3, mod, g)


def kernel(x_prompt, x_sample, c, cache_k, cache_v, state_lru, c_ctx, w_mod, b_mod, g_pre, g_post,
           w_in_attn, q_norm, k_norm, w_out_attn, w_in_lru, conv_w, conv_b,
           w_rg_a, b_rg_a, w_rg_x, b_rg_x, lru_lambda, w_out_lru):
    n_p, t_p, d = x_prompt.shape
    n_s, t_s, _ = x_sample.shape
    depth = w_mod.shape[0]
    n_ctx = cache_k.shape[2]

    cond = jnp.concatenate([c_ctx[None, :], c, jnp.zeros((COND_ROWS - 1 - n_s, d), F32)], axis=0)
    mod = _modulation(cond, w_mod, b_mod)

    y_p = x_prompt
    y_s = x_sample
    new_k, new_v, new_h = [], [], []
    for l in range(depth):
        j = l // 2
        mod_p = mod[l, 0:1].reshape(1, 1, 3 * d)
        mod_s = mod[l, 1:1 + n_s].reshape(n_s, 1, 3 * d)
        g_pre_l = g_pre[l].reshape(1, d)
        g_post_l = g_post[l].reshape(1, d)
        if l % 2 == 0:
            w_in = w_in_attn[j].astype(BF16)
            w_out = w_out_attn[j].astype(BF16)
            qn = q_norm[j].reshape(1, HEAD_DIM)
            kn = k_norm[j].reshape(1, HEAD_DIM)
            yp2 = y_p.reshape(n_p * t_p, d)
            ys2 = y_s.reshape(n_s * t_s, d)
            pp = _inproj(yp2, mod_p, g_pre_l, w_in, rows_per_mod=n_p * t_p)
            ps = _inproj(ys2, mod_s, g_pre_l, w_in, rows_per_mod=t_s)
            og_p, k_p, v_p = _attention(pp, n_p, t_p, qn, kn, n_seq=4)
            ck = cache_k[:, j].reshape(n_s * n_ctx, KV_WIDTH)
            cv = cache_v[:, j].reshape(n_s * n_ctx, KV_WIDTH)
            og_s = _attention(ps, n_s, t_s, qn, kn, cache=(ck, cv), rope=_rope_tables(t_s))
            y_p, y_s = _outproj_pair(og_p, og_s, w_out, yp2, ys2, mod_p, mod_s, g_post_l, rows_per_mod_b=t_s)
            y_p = y_p.reshape(n_p, t_p, d)
            y_s = y_s.reshape(n_s, t_s, d)
            new_k.append(k_p.reshape(n_p, t_p, N_KV_HEADS, HEAD_DIM))
            new_v.append(v_p.reshape(n_p, t_p, N_KV_HEADS, HEAD_DIM))
        else:
            w_in = w_in_lru[j].astype(BF16)
            w_out = w_out_lru[j].astype(BF16)
            cb = conv_b[j].reshape(1, LRU_WIDTH)
            lru_args = (conv_w[j], cb, w_rg_a[j], b_rg_a[j], w_rg_x[j], b_rg_x[j], lru_lambda[j])
            xg_p = _inproj_perm(y_p, mod_p, g_pre_l, w_in)
            xg_s = _inproj_perm(y_s, mod_s, g_pre_l, w_in)
            h0_p = jnp.zeros((2, n_p, LRU_WIDTH), F32)
            h0_s = jnp.swapaxes(state_lru[:, j], 0, 1)
            h0_s = jnp.concatenate([h0_s, h0_s], axis=1)
            yg_p, fin = _lru_core(xg_p, n_p, t_p, *lru_args, h0_p, with_final=True)
            yg_s = _lru_core(xg_s, n_s, t_s, *lru_args, h0_s, with_final=False)
            y_p = _outproj_perm(yg_p, w_out, y_p, mod_p, g_post_l)
            y_s = _outproj_perm(yg_s, w_out, y_s, mod_s, g_post_l)
            new_h.append(jnp.swapaxes(fin, 0, 1))
    new_cache_k = jnp.stack(new_k, axis=1)
    new_cache_v = jnp.stack(new_v, axis=1)
    new_state_lru = jnp.stack(new_h, axis=1)
    return (y_p, y_s, new_cache_k, new_cache_v, new_state_lru)
```

```python
import functools

import jax
import jax.numpy as jnp
from jax import lax
from jax.experimental import pallas as pl
from jax.experimental.pallas import tpu as pltpu

F32 = jnp.float32
BF16 = jnp.bfloat16

D_MODEL = 2048
HEAD_DIM = 128
N_HEADS = 16
N_KV_HEADS = 4
Q_PER_KV = N_HEADS // N_KV_HEADS
ATTN_WIDTH = N_HEADS * HEAD_DIM
KV_WIDTH = N_KV_HEADS * HEAD_DIM
GRID_W = 64
AXIS_DIM = HEAD_DIM // 2
ROPE_THETA = 10000.0
LRU_WIDTH = D_MODEL
LRU_BLOCKS = 8
LRU_BLOCK_DIM = LRU_WIDTH // LRU_BLOCKS
CONV_WIDTH = 4
CONV_LEFT = (CONV_WIDTH - 1) // 2
CONV_RIGHT = CONV_WIDTH - 1 - CONV_LEFT
RG_C = 8.0
EPS = 1e-6
LOG2_E = 1.4426950408889634

LANES = 128
N_SLABS = D_MODEL // LANES
COND_ROWS = 8
VMEM_LIMIT = 60 * 1024 * 1024


def _params(sem):
    return pltpu.CompilerParams(dimension_semantics=sem, vmem_limit_bytes=VMEM_LIMIT)


def _row_loop(n_rows, chunk, body, unroll=1):
    def step(i, carry):
        body(pl.multiple_of(i * chunk, chunk))
        return carry
    lax.fori_loop(0, n_rows // chunk, step, 0, unroll=unroll)


def _rms(x, width):
    return x * lax.rsqrt(jnp.sum(x * x, axis=-1, keepdims=True) * (1.0 / width) + EPS)


def _sigmoid(x):
    return 0.5 * jnp.tanh(0.5 * x) + 0.5


def _silu(x):
    return x * _sigmoid(x)


def _mod_kernel(cond_ref, w_ref, b_ref, o_ref):
    x = _silu(cond_ref[...]).astype(BF16)
    o_ref[0] = jnp.dot(x, w_ref[0].astype(BF16), preferred_element_type=F32) + b_ref[0]


def _modulation(cond, w_mod, b_mod, tn=1024):
    depth, d, n = w_mod.shape
    return pl.pallas_call(
        _mod_kernel,
        grid=(depth, n // tn),
        in_specs=[
            pl.BlockSpec((COND_ROWS, d), lambda l, j: (0, 0)),
            pl.BlockSpec((1, d, tn), lambda l, j: (l, 0, j)),
            pl.BlockSpec((1, 1, tn), lambda l, j: (l, 0, j)),
        ],
        out_specs=pl.BlockSpec((1, COND_ROWS, tn), lambda l, j: (l, 0, j)),
        out_shape=jax.ShapeDtypeStruct((depth, COND_ROWS, n), F32),
        compiler_params=_params(("arbitrary", "arbitrary")),
        name="modulation",
    )(cond, w_mod, b_mod.reshape(depth, 1, n))


def _norm_mod(x, g, scale1, shift):
    return (_rms(x, D_MODEL) * g) * scale1 + shift


def _inproj_kernel(y_ref, mod_ref, g_ref, w_ref, o_ref, h_scr, *, tm, chunk):
    @pl.when(pl.program_id(1) == 0)
    def _():
        g = g_ref[...]
        shift = mod_ref[0, :, 0:D_MODEL]
        scale1 = 1.0 + mod_ref[0, :, D_MODEL:2 * D_MODEL]

        def body(r0):
            x = y_ref[pl.ds(r0, chunk), :]
            h_scr[pl.ds(r0, chunk), :] = _norm_mod(x, g, scale1, shift).astype(BF16)
        _row_loop(tm, chunk, body, unroll=4)

    o_ref[...] = jnp.dot(h_scr[...], w_ref[...], preferred_element_type=F32)


def _inproj(y, mod, g, w, rows_per_mod, tm=1024, tn=1024, chunk=32):
    rows, d = y.shape
    n = w.shape[1]
    tiles_per_mod = rows_per_mod // tm
    return pl.pallas_call(
        functools.partial(_inproj_kernel, tm=tm, chunk=chunk),
        grid=(rows // tm, n // tn),
        in_specs=[
            pl.BlockSpec((tm, d), lambda i, j: (i, 0)),
            pl.BlockSpec((1, 1, 3 * d), lambda i, j: (i // tiles_per_mod, 0, 0)),
            pl.BlockSpec((1, d), lambda i, j: (0, 0)),
            pl.BlockSpec((d, tn), lambda i, j: (0, j)),
        ],
        out_specs=pl.BlockSpec((tm, tn), lambda i, j: (i, j)),
        out_shape=jax.ShapeDtypeStruct((rows, n), F32),
        scratch_shapes=[pltpu.VMEM((tm, d), BF16)],
        compiler_params=_params(("arbitrary", "arbitrary")),
        name="attn_inproj",
    )(y, mod, g, w)


ATTN_SUB_ROWS = 256


def _rope(x, cos, sin_lo, sin_hi):
    return x * cos + pltpu.roll(x, HEAD_DIM - AXIS_DIM // 2, 1) * sin_lo + pltpu.roll(x, AXIS_DIM // 2, 1) * sin_hi


def _attn_kernel(*refs, t_len, tq, n_ctx, n_seq, latent):
    if latent:
        (q_ref, k_ref, v_ref, g_ref, ck_ref, cv_ref, qn_ref, kn_ref,
         cosq_ref, sloq_ref, shiq_ref, cosk_ref, slok_ref, shik_ref,
         og_ref, k_scr, v_scr) = refs
    else:
        (q_ref, k_ref, v_ref, g_ref, qn_ref, kn_ref,
         og_ref, ko_ref, vo_ref, k_scr, v_scr) = refs

    kv = pl.program_id(1)

    @pl.when(pl.program_id(2) == 0)
    def _():
        for s in range(n_seq):
            rows = slice(s * t_len, (s + 1) * t_len)
            kn = _rms(k_ref[rows, :], HEAD_DIM) * kn_ref[...]
            v = v_ref[rows, :]
            if latent:
                kn = _rope(kn, cosk_ref[...], slok_ref[...], shik_ref[...])
                ctx_rows = pl.ds(kv, n_ctx, stride=N_KV_HEADS)
                k_scr[s, 0:n_ctx, :] = ck_ref[ctx_rows, :].astype(BF16)
                v_scr[s, 0:n_ctx, :] = cv_ref[ctx_rows, :].astype(BF16)
            else:
                new_rows = pl.ds(s * t_len * N_KV_HEADS + kv, t_len, stride=N_KV_HEADS)
                ko_ref[new_rows, :] = kn
                vo_ref[new_rows, :] = v
            k_scr[s, n_ctx:n_ctx + t_len, :] = kn.astype(BF16)
            v_scr[s, n_ctx:n_ctx + t_len, :] = v.astype(BF16)

    qn = qn_ref[...]
    q_scale = (HEAD_DIM ** -0.5) * LOG2_E
    chains = [(s, s * tq + r0, slice(h * HEAD_DIM, (h + 1) * HEAD_DIM))
              for s in range(n_seq) for r0 in range(0, tq, ATTN_SUB_ROWS) for h in range(Q_PER_KV)]

    def scores(s, r0, cols):
        rows = slice(r0, r0 + ATTN_SUB_ROWS)
        qh = _rms(q_ref[rows, cols], HEAD_DIM) * qn
        if latent:
            qh = _rope(qh, cosq_ref[rows, :], sloq_ref[rows, :], shiq_ref[rows, :])
        qh = (qh * q_scale).astype(BF16)
        return lax.dot_general(qh, k_scr[s], (((1,), (1,)), ((), ())), preferred_element_type=F32)

    sc_next = scores(*chains[0])
    for i, (s, r0, cols) in enumerate(chains):
        rows = slice(r0, r0 + ATTN_SUB_ROWS)
        sc = sc_next
        if i + 1 < len(chains):
            sc_next = scores(*chains[i + 1])
        e = jnp.exp2(sc - jnp.max(sc, axis=-1, keepdims=True))
        denom = jnp.sum(e, axis=-1, keepdims=True)
        o = jnp.dot(e.astype(BF16), v_scr[s], preferred_element_type=F32) / denom
        og_ref[rows, cols] = (o * _silu(g_ref[rows, cols])).astype(BF16)


def _attention(p, n_batch, t_len, q_norm, k_norm, cache=None, rope=None, tq=512, n_seq=1):
    tq = min(tq, t_len)
    assert n_seq == 1 or tq == t_len
    rows = n_batch * t_len
    nq = t_len // tq
    gw = Q_PER_KV * HEAD_DIM
    k_col = ATTN_WIDTH // HEAD_DIM
    v_col = (ATTN_WIDTH + KV_WIDTH) // HEAD_DIM
    g_col = (ATTN_WIDTH + 2 * KV_WIDTH) // gw
    latent = cache is not None
    n_ctx = cache[0].shape[0] // (n_batch * N_KV_HEADS) if latent else 0

    in_specs = [
        pl.BlockSpec((n_seq * tq, gw), lambda b, kv, qi: (b * nq + qi, kv)),
        pl.BlockSpec((n_seq * t_len, HEAD_DIM), lambda b, kv, qi: (b, k_col + kv)),
        pl.BlockSpec((n_seq * t_len, HEAD_DIM), lambda b, kv, qi: (b, v_col + kv)),
        pl.BlockSpec((n_seq * tq, gw), lambda b, kv, qi: (b * nq + qi, g_col + kv)),
    ]
    args = [p, p, p, p]
    if latent:
        in_specs += [pl.BlockSpec((n_ctx * N_KV_HEADS, HEAD_DIM), lambda b, kv, qi: (b, 0))] * 2
        args += list(cache)
    in_specs += [pl.BlockSpec((1, HEAD_DIM), lambda b, kv, qi: (0, 0))] * 2
    args += [q_norm, k_norm]
    og_spec = pl.BlockSpec((n_seq * tq, gw), lambda b, kv, qi: (b * nq + qi, kv))
    og_shape = jax.ShapeDtypeStruct((rows, ATTN_WIDTH), BF16)
    if latent:
        in_specs += [pl.BlockSpec((tq, HEAD_DIM), lambda b, kv, qi: (qi, 0))] * 3
        in_specs += [pl.BlockSpec((t_len, HEAD_DIM), lambda b, kv, qi: (0, 0))] * 3
        args += list(rope) + list(rope)
        out_specs = og_spec
        out_shape = og_shape
    else:
        kv_spec = pl.BlockSpec((n_seq * t_len * N_KV_HEADS, HEAD_DIM), lambda b, kv, qi: (b, 0))
        kv_shape = jax.ShapeDtypeStruct((rows * N_KV_HEADS, HEAD_DIM), F32)
        out_specs = (og_spec, kv_spec, kv_spec)
        out_shape = (og_shape, kv_shape, kv_shape)
    return pl.pallas_call(
        functools.partial(_attn_kernel, t_len=t_len, tq=tq, n_ctx=n_ctx, n_seq=n_seq, latent=latent),
        grid=(n_batch // n_seq, N_KV_HEADS, nq),
        in_specs=in_specs,
        out_specs=out_specs,
        out_shape=out_shape,
        scratch_shapes=[pltpu.VMEM((n_seq, n_ctx + t_len, HEAD_DIM), BF16)] * 2,
        compiler_params=_params(("arbitrary", "arbitrary", "arbitrary")),
        name="attn_latent" if latent else "attn_context",
    )(*args)


def _rope_tables(n):
    rows = n // GRID_W
    row = jnp.repeat(jnp.arange(rows, dtype=F32), GRID_W)
    col = jnp.tile(jnp.arange(GRID_W, dtype=F32), rows)
    inv = ROPE_THETA ** (-jnp.arange(0, AXIS_DIM, 2, dtype=F32) / AXIS_DIM)
    ar = row[:, None] * inv
    ac = col[:, None] * inv
    ang = jnp.concatenate([ar, ar, ac, ac], axis=-1)
    cos, sin = jnp.cos(ang), jnp.sin(ang)
    first = (jnp.arange(HEAD_DIM) % AXIS_DIM) < AXIS_DIM // 2
    return cos, jnp.where(first, -sin, 0.0), jnp.where(first, 0.0, sin)


OUT_SUB_ROWS = 256


def _outproj_kernel(x_ref, w_ref, y_ref, mod_ref, g_ref, o_ref, *, tm, chunk):
    g = g_ref[...]
    gate = mod_ref[0, :, 2 * D_MODEL:3 * D_MODEL]
    def project(r0):
        return jnp.dot(x_ref[r0:r0 + OUT_SUB_ROWS, :], w_ref[...], preferred_element_type=F32)

    m_next = project(0)
    for r0 in range(0, tm, OUT_SUB_ROWS):
        m = m_next
        if r0 + OUT_SUB_ROWS < tm:
            m_next = project(r0 + OUT_SUB_ROWS)
        for c0 in range(0, OUT_SUB_ROWS, chunk):
            rows = slice(r0 + c0, r0 + c0 + chunk)
            o_ref[rows, :] = y_ref[rows, :] + gate * (_rms(m[c0:c0 + chunk, :], D_MODEL) * g)


def _outproj_pair_kernel(xa_ref, xb_ref, w_ref, ya_ref, yb_ref, moda_ref, modb_ref, g_ref, oa_ref, ob_ref,
                         *, tiles_a, tm, chunk):
    @pl.when(pl.program_id(0) < tiles_a)
    def _():
        _outproj_kernel(xa_ref, w_ref, ya_ref, moda_ref, g_ref, oa_ref, tm=tm, chunk=chunk)

    @pl.when(pl.program_id(0) >= tiles_a)
    def _():
        _outproj_kernel(xb_ref, w_ref, yb_ref, modb_ref, g_ref, ob_ref, tm=tm, chunk=chunk)


def _outproj_pair(xa, xb, w, ya, yb, moda, modb, g, rows_per_mod_b, tm=512, chunk=32):
    d = ya.shape[1]
    k = xa.shape[1]
    tiles_a = ya.shape[0] // tm
    tiles_b = yb.shape[0] // tm
    tiles_per_mod = rows_per_mod_b // tm
    ia = lambda i: (jnp.minimum(i, tiles_a - 1), 0)
    ib = lambda i: (jnp.maximum(i - tiles_a, 0), 0)
    return pl.pallas_call(
        functools.partial(_outproj_pair_kernel, tiles_a=tiles_a, tm=tm, chunk=chunk),
        grid=(tiles_a + tiles_b,),
        in_specs=[
            pl.BlockSpec((tm, k), ia),
            pl.BlockSpec((tm, k), ib),
            pl.BlockSpec((k, d), lambda i: (0, 0), pipeline_mode=pl.Buffered(1)),
            pl.BlockSpec((tm, d), ia),
            pl.BlockSpec((tm, d), ib),
            pl.BlockSpec((1, 1, 3 * d), lambda i: (0, 0, 0)),
            pl.BlockSpec((1, 1, 3 * d), lambda i: (jnp.maximum(i - tiles_a, 0) // tiles_per_mod, 0, 0)),
            pl.BlockSpec((1, d), lambda i: (0, 0)),
        ],
        out_specs=(pl.BlockSpec((tm, d), ia), pl.BlockSpec((tm, d), ib)),
        out_shape=(jax.ShapeDtypeStruct(ya.shape, F32), jax.ShapeDtypeStruct(yb.shape, F32)),
        compiler_params=_params(("arbitrary",)),
        name="attn_outproj",
    )(xa, xb, w, ya, yb, moda, modb, g)


PERM_ROWS = 32


def _inproj_perm_kernel(y_ref, mod_ref, g_ref, w_ref, o_ref, slab_scr, h_scr, *, n_batch, tt, n_mod):
    @pl.when(pl.program_id(1) == 0)
    def _():
        g = g_ref[...]
        for b in range(n_batch):
            mb = mod_ref[b if n_mod > 1 else 0]
            shift = mb[:, 0:D_MODEL]
            scale1 = 1.0 + mb[:, D_MODEL:2 * D_MODEL]
            for r0 in range(0, tt, PERM_ROWS):
                h = _norm_mod(y_ref[b, r0:r0 + PERM_ROWS, :], g, scale1, shift)
                for c in range(N_SLABS):
                    slab_scr[c, pl.ds(r0 * n_batch + b, PERM_ROWS, stride=n_batch), :] = (
                        h[:, c * LANES:(c + 1) * LANES])
        for c in range(N_SLABS):
            h_scr[:, c * LANES:(c + 1) * LANES] = slab_scr[c].astype(BF16)

    o_ref[...] = jnp.dot(h_scr[...], w_ref[...], preferred_element_type=F32)


def _inproj_perm(y3, mod, g, w, tm=1024, tn=1024):
    n_batch, t_len, d = y3.shape
    n = w.shape[1]
    tt = tm // n_batch
    n_mod = mod.shape[0]
    return pl.pallas_call(
        functools.partial(_inproj_perm_kernel, n_batch=n_batch, tt=tt, n_mod=n_mod),
        grid=(t_len // tt, n // tn),
        in_specs=[
            pl.BlockSpec((n_batch, tt, d), lambda i, j: (0, i, 0)),
            pl.BlockSpec((n_mod, 1, 3 * d), lambda i, j: (0, 0, 0)),
            pl.BlockSpec((1, d), lambda i, j: (0, 0)),
            pl.BlockSpec((d, tn), lambda i, j: (0, j)),
        ],
        out_specs=pl.BlockSpec((tm, tn), lambda i, j: (i, j)),
        out_shape=jax.ShapeDtypeStruct((n_batch * t_len, n), F32),
        scratch_shapes=[pltpu.VMEM((N_SLABS, tm, LANES), F32), pltpu.VMEM((tm, d), BF16)],
        compiler_params=_params(("arbitrary", "arbitrary")),
        name="lru_inproj",
    )(y3, mod, g, w)


SCAN_STEPS = 8


def _log_sigmoid(x):
    return -(jnp.maximum(-x, 0.0) + jnp.log1p(jnp.exp(-jnp.abs(x))))


def _lru_core_kernel(*refs, n_batch, t_len, chunk, with_final):
    (xb_ref, gb_ref, cw_ref, cb_ref, wa_ref, ba_ref, wx_ref, bx_ref, lam_ref, h0_ref) = refs[:10]
    if with_final:
        y_ref, fin_ref, a_scr, b_scr, xc_scr, hb_scr, w16_scr = refs[10:]
    else:
        y_ref, a_scr, b_scr, xc_scr, hb_scr, w16_scr = refs[10:]
    rows = n_batch * t_len
    c = LRU_BLOCK_DIM
    pad_lo = CONV_LEFT * n_batch
    pad_hi = CONV_RIGHT * n_batch

    a_scr[1, 0:pad_lo, :] = jnp.zeros((pad_lo, c), F32)
    a_scr[1, pad_lo + rows:pad_lo + rows + pad_hi, :] = jnp.zeros((pad_hi, c), F32)
    for r0 in range(0, rows, chunk):
        a_scr[1, pad_lo + r0:pad_lo + r0 + chunk, :] = xb_ref[r0:r0 + chunk, :]
    for r0 in range(0, rows, chunk):
        acc = cb_ref[...]
        for j in range(CONV_WIDTH):
            acc = acc + a_scr[1, r0 + j * n_batch:r0 + j * n_batch + chunk, :] * cw_ref[j:j + 1, :]
        xc_scr[r0:r0 + chunk, :] = acc

    for d in range(2):
        w16_scr[2 * d] = (0.5 * wa_ref[d, 0]).astype(BF16)
        w16_scr[2 * d + 1] = (0.5 * wx_ref[d, 0]).astype(BF16)
    half_ba = [0.5 * ba_ref[d:d + 1, :] for d in range(2)]
    half_bx = [0.5 * bx_ref[d:d + 1, :] for d in range(2)]
    neg_log_s = [-(RG_C * _log_sigmoid(lam_ref[d:d + 1, :])) for d in range(2)]
    exp2_s = [-LOG2_E * neg_log_s[d] for d in range(2)]

    def coeffs(r0):
        x = xc_scr[pl.ds(r0, chunk), :]
        x16 = x.astype(BF16)
        for d in range(2):
            r = 0.5 * jnp.tanh(jnp.dot(x16, w16_scr[2 * d], preferred_element_type=F32) + half_ba[d]) + 0.5
            i = 0.5 * jnp.tanh(jnp.dot(x16, w16_scr[2 * d + 1], preferred_element_type=F32) + half_bx[d]) + 0.5
            a = jnp.exp2(r * exp2_s[d])
            m2 = jnp.tanh(r * neg_log_s[d]) * (a * a + 1.0)
            mult = jnp.where(m2 > 0.0, m2 * lax.rsqrt(m2), 0.0)
            a_scr[d, pl.ds(r0, chunk), :] = a
            b_scr[d, pl.ds(r0, chunk), :] = (mult * i) * x
    _row_loop(rows, chunk, coeffs, unroll=2)

    hf_scr = xc_scr
    if n_batch == 16:
        blk = SCAN_STEPS * 16

        def steps(kb, carry):
            hf, hb = carry
            base_f = pl.multiple_of(kb * blk, blk)
            base_b = pl.multiple_of(rows - blk - kb * blk, blk)
            af = a_scr[0, pl.ds(base_f, blk), :]
            bf = b_scr[0, pl.ds(base_f, blk), :]
            ab = a_scr[1, pl.ds(base_b, blk), :]
            bb = b_scr[1, pl.ds(base_b, blk), :]
            hfs, hbs = [], []
            for j in range(SCAN_STEPS):
                rf = slice(16 * j, 16 * (j + 1))
                rb = slice(blk - 16 * (j + 1), blk - 16 * j)
                hf = af[rf] * hf + bf[rf]
                hb = ab[rb] * hb + bb[rb]
                hfs.append(hf)
                hbs.append(hb)
            hf_scr[pl.ds(base_f, blk), :] = jnp.concatenate(hfs, axis=0)
            hb_scr[pl.ds(base_b, blk), :] = jnp.concatenate(hbs[::-1], axis=0)
            return hf, hb
        hf, hb = lax.fori_loop(0, t_len // SCAN_STEPS, steps, (h0_ref[0], h0_ref[1]))
        if with_final:
            fin_ref[0] = hf
            fin_ref[1] = hb
    else:
        lower = lax.broadcasted_iota(jnp.int32, (8, c), 0) < 4

        blk = SCAN_STEPS * 8

        def steps(kb, carry):
            cf, cb = carry
            base_f = pl.multiple_of(kb * blk, blk)
            base_b = pl.multiple_of(rows - blk - kb * blk, blk)
            for j in range(SCAN_STEPS):
                rf = pl.ds(base_f + 8 * j, 8)
                rb = pl.ds(base_b + 8 * (SCAN_STEPS - 1 - j), 8)
                af = a_scr[0, rf, :]
                bf = b_scr[0, rf, :]
                ab = a_scr[1, rb, :]
                bb = b_scr[1, rb, :]
                uf = af * cf + bf
                ub = ab * cb + bb
                cf = pltpu.roll(af, 4, 0) * uf + pltpu.roll(bf, 4, 0)
                cb = pltpu.roll(ab, 4, 0) * ub + pltpu.roll(bb, 4, 0)
                hf_scr[rf, :] = jnp.where(lower, uf, pltpu.roll(cf, 4, 0))
                hb_scr[rb, :] = jnp.where(lower, pltpu.roll(cb, 4, 0), ub)
            return cf, cb
        lax.fori_loop(0, rows // blk, steps, (h0_ref[0], h0_ref[1]))

    def emit(r0):
        h = hf_scr[pl.ds(r0, chunk), :] + hb_scr[pl.ds(r0, chunk), :]
        y_ref[pl.ds(r0, chunk), :] = (h * _silu(gb_ref[pl.ds(r0, chunk), :])).astype(BF16)
    _row_loop(rows, chunk, emit)


def _lru_core(xg, n_batch, t_len, conv_w, conv_b, w_a, b_a, w_x, b_x, lam, h0, with_final, chunk=256):
    rows = n_batch * t_len
    c = LRU_BLOCK_DIM
    h_rows = h0.shape[1]
    pad = (CONV_WIDTH - 1) * n_batch
    in_specs = [
        pl.BlockSpec((rows, c), lambda i: (0, i)),
        pl.BlockSpec((rows, c), lambda i: (0, LRU_BLOCKS + i)),
        pl.BlockSpec((CONV_WIDTH, c), lambda i: (0, i)),
        pl.BlockSpec((1, c), lambda i: (0, i)),
        pl.BlockSpec((2, 1, c, c), lambda i: (0, i, 0, 0)),
        pl.BlockSpec((2, c), lambda i: (0, i)),
        pl.BlockSpec((2, 1, c, c), lambda i: (0, i, 0, 0)),
        pl.BlockSpec((2, c), lambda i: (0, i)),
        pl.BlockSpec((2, c), lambda i: (0, i)),
        pl.BlockSpec((2, h_rows, c), lambda i: (0, 0, i)),
    ]
    y_spec = pl.BlockSpec((rows, c), lambda i: (0, i))
    y_shape = jax.ShapeDtypeStruct((rows, LRU_WIDTH), BF16)
    if with_final:
        out_specs = (y_spec, pl.BlockSpec((2, n_batch, c), lambda i: (0, 0, i)))
        out_shape = (y_shape, jax.ShapeDtypeStruct((2, n_batch, LRU_WIDTH), F32))
    else:
        out_specs = y_spec
        out_shape = y_shape
    return pl.pallas_call(
        functools.partial(_lru_core_kernel, n_batch=n_batch, t_len=t_len, chunk=chunk, with_final=with_final),
        grid=(LRU_BLOCKS,),
        in_specs=in_specs,
        out_specs=out_specs,
        out_shape=out_shape,
        scratch_shapes=[
            pltpu.VMEM((2, rows + pad, c), F32),
            pltpu.VMEM((2, rows, c), F32),
            pltpu.VMEM((rows, c), F32),
            pltpu.VMEM((rows, c), F32),
            pltpu.VMEM((4, c, c), BF16),
        ],
        compiler_params=_params(("arbitrary",)),
        name="lru_core",
    )(xg, xg, conv_w, conv_b, w_a, b_a, w_x, b_x, lam, h0)


MXU_COLS = 256


def _outproj_perm_kernel(x_ref, w_ref, y_ref, mod_ref, g_ref, o_ref, *slab_scrs, n_batch, tt, n_mod):
    g = g_ref[...]
    sub_t = OUT_SUB_ROWS // n_batch
    perm = min(PERM_ROWS, sub_t)
    def project(blk):
        rows = slice(blk * OUT_SUB_ROWS, (blk + 1) * OUT_SUB_ROWS)
        for n0 in range(0, D_MODEL, MXU_COLS):
            m = jnp.dot(x_ref[rows, :], w_ref[:, n0:n0 + MXU_COLS], preferred_element_type=F32)
            for c in range(MXU_COLS // LANES):
                slab_scrs[blk][n0 // LANES + c] = m[:, c * LANES:(c + 1) * LANES]

    project(0)
    for blk, slab_scr in enumerate(slab_scrs):
        if blk + 1 < len(slab_scrs):
            project(blk + 1)
        t0 = blk * sub_t
        for b in range(n_batch):
            gate = mod_ref[b if n_mod > 1 else 0][:, 2 * D_MODEL:3 * D_MODEL]
            for r0 in range(0, sub_t, perm):
                m = jnp.concatenate(
                    [slab_scr[c, pl.ds(r0 * n_batch + b, perm, stride=n_batch), :] for c in range(N_SLABS)],
                    axis=-1)
                out_rows = slice(t0 + r0, t0 + r0 + perm)
                o_ref[b, out_rows, :] = y_ref[b, out_rows, :] + gate * (_rms(m, D_MODEL) * g)


def _outproj_perm(x, w, y3, mod, g, tm=512):
    n_batch, t_len, d = y3.shape
    k = x.shape[1]
    tt = tm // n_batch
    n_mod = mod.shape[0]
    return pl.pallas_call(
        functools.partial(_outproj_perm_kernel, n_batch=n_batch, tt=tt, n_mod=n_mod),
        grid=(t_len // tt,),
        in_specs=[
            pl.BlockSpec((tm, k), lambda i: (i, 0)),
            pl.BlockSpec((k, d), lambda i: (0, 0)),
            pl.BlockSpec((n_batch, tt, d), lambda i: (0, i, 0)),
            pl.BlockSpec((n_mod, 1, 3 * d), lambda i: (0, 0, 0)),
            pl.BlockSpec((1, d), lambda i: (0, 0)),
        ],
        out_specs=pl.BlockSpec((n_batch, tt, d), lambda i: (0, i, 0)),
        out_shape=jax.ShapeDtypeStruct((n_batch, t_len, d), F32),
        scratch_shapes=[pltpu.VMEM((N_SLABS, OUT_SUB_ROWS, LANES), F32)] * (tm // OUT_SUB_ROWS),
        compiler_params=_params(("arbitrary",)),
        name="lru_outproj",
    )(x, w, y3, mod, g)


def kernel(x_prompt, x_sample, c, cache_k, cache_v, state_lru, c_ctx, w_mod, b_mod, g_pre, g_post,
           w_in_attn, q_norm, k_norm, w_out_attn, w_in_lru, conv_w, conv_b,
           w_rg_a, b_rg_a, w_rg_x, b_rg_x, lru_lambda, w_out_lru):
    n_p, t_p, d = x_prompt.shape
    n_s, t_s, _ = x_sample.shape
    depth = w_mod.shape[0]
    n_ctx = cache_k.shape[2]

    cond = jnp.concatenate([c_ctx[None, :], c, jnp.zeros((COND_ROWS - 1 - n_s, d), F32)], axis=0)
    mod = _modulation(cond, w_mod, b_mod)

    y_p = x_prompt
    y_s = x_sample
    new_k, new_v, new_h = [], [], []
    for l in range(depth):
        j = l // 2
        mod_p = mod[l, 0:1].reshape(1, 1, 3 * d)
        mod_s = mod[l, 1:1 + n_s].reshape(n_s, 1, 3 * d)
        g_pre_l = g_pre[l].reshape(1, d)
        g_post_l = g_post[l].reshape(1, d)
        if l % 2 == 0:
            w_in = w_in_attn[j].astype(BF16)
            w_out = w_out_attn[j].astype(BF16)
            qn = q_norm[j].reshape(1, HEAD_DIM)
            kn = k_norm[j].reshape(1, HEAD_DIM)
            yp2 = y_p.reshape(n_p * t_p, d)
            ys2 = y_s.reshape(n_s * t_s, d)
            pp = _inproj(yp2, mod_p, g_pre_l, w_in, rows_per_mod=n_p * t_p)
            ps = _inproj(ys2, mod_s, g_pre_l, w_in, rows_per_mod=t_s)
            og_p, k_p, v_p = _attention(pp, n_p, t_p, qn, kn, n_seq=4)
            ck = cache_k[:, j].reshape(n_s * n_ctx * N_KV_HEADS, HEAD_DIM)
            cv = cache_v[:, j].reshape(n_s * n_ctx * N_KV_HEADS, HEAD_DIM)
            og_s = _attention(ps, n_s, t_s, qn, kn, cache=(ck, cv), rope=_rope_tables(t_s))
            y_p, y_s = _outproj_pair(og_p, og_s, w_out, yp2, ys2, mod_p, mod_s, g_post_l, rows_per_mod_b=t_s)
            y_p = y_p.reshape(n_p, t_p, d)
            y_s = y_s.reshape(n_s, t_s, d)
            new_k.append(k_p.reshape(n_p, t_p, N_KV_HEADS, HEAD_DIM))
            new_v.append(v_p.reshape(n_p, t_p, N_KV_HEADS, HEAD_DIM))
        else:
            w_in = w_in_lru[j].astype(BF16)
            w_out = w_out_lru[j].astype(BF16)
            cb = conv_b[j].reshape(1, LRU_WIDTH)
            lru_args = (conv_w[j], cb, w_rg_a[j], b_rg_a[j], w_rg_x[j], b_rg_x[j], lru_lambda[j])
            xg_p = _inproj_perm(y_p, mod_p, g_pre_l, w_in)
            xg_s = _inproj_perm(y_s, mod_s, g_pre_l, w_in)
            h0_p = jnp.zeros((2, n_p, LRU_WIDTH), F32)
            h0_s = jnp.swapaxes(state_lru[:, j], 0, 1)
            h0_s = jnp.concatenate([h0_s, h0_s], axis=1)
            yg_p, fin = _lru_core(xg_p, n_p, t_p, *lru_args, h0_p, with_final=True)
            yg_s = _lru_core(xg_s, n_s, t_s, *lru_args, h0_s, with_final=False)
            y_p = _outproj_perm(yg_p, w_out, y_p, mod_p, g_post_l)
            y_s = _outproj_perm(yg_s, w_out, y_s, mod_s, g_post_l)
            new_h.append(jnp.swapaxes(fin, 0, 1))
    new_cache_k = jnp.stack(new_k, axis=1)
    new_cache_v = jnp.stack(new_v, axis=1)
    new_state_lru = jnp.stack(new_h, axis=1)
    return (y_p, y_s, new_cache_k, new_cache_v, new_state_lru)
```

```python
import functools

import jax
import jax.numpy as jnp
from jax import lax
from jax.experimental import pallas as pl
from jax.experimental.pallas import tpu as pltpu

F32 = jnp.float32
BF16 = jnp.bfloat16

D_MODEL = 2048
HEAD_DIM = 128
N_HEADS = 16
N_KV_HEADS = 4
Q_PER_KV = N_HEADS // N_KV_HEADS
ATTN_WIDTH = N_HEADS * HEAD_DIM
KV_WIDTH = N_KV_HEADS * HEAD_DIM
GRID_W = 64
AXIS_DIM = HEAD_DIM // 2
ROPE_THETA = 10000.0
LRU_WIDTH = D_MODEL
LRU_BLOCKS = 8
LRU_BLOCK_DIM = LRU_WIDTH // LRU_BLOCKS
CONV_WIDTH = 4
CONV_LEFT = (CONV_WIDTH - 1) // 2
CONV_RIGHT = CONV_WIDTH - 1 - CONV_LEFT
RG_C = 8.0
EPS = 1e-6
LOG2_E = 1.4426950408889634

LANES = 128
N_SLABS = D_MODEL // LANES
COND_ROWS = 8
VMEM_LIMIT = 60 * 1024 * 1024


def _params(sem):
    return pltpu.CompilerParams(dimension_semantics=sem, vmem_limit_bytes=VMEM_LIMIT)


def _row_loop(n_rows, chunk, body, unroll=1):
    def step(i, carry):
        body(pl.multiple_of(i * chunk, chunk))
        return carry
    lax.fori_loop(0, n_rows // chunk, step, 0, unroll=unroll)


def _rms(x, width):
    return x * lax.rsqrt(jnp.sum(x * x, axis=-1, keepdims=True) * (1.0 / width) + EPS)


def _sigmoid(x):
    return 0.5 * jnp.tanh(0.5 * x) + 0.5


def _silu(x):
    return x * _sigmoid(x)


def _mod_kernel(cond_ref, w_ref, b_ref, o_ref):
    x = _silu(cond_ref[...]).astype(BF16)
    o_ref[0] = jnp.dot(x, w_ref[0].astype(BF16), preferred_element_type=F32) + b_ref[0]


def _modulation(cond, w_mod, b_mod, tn=1024):
    depth, d, n = w_mod.shape
    return pl.pallas_call(
        _mod_kernel,
        grid=(depth, n // tn),
        in_specs=[
            pl.BlockSpec((COND_ROWS, d), lambda l, j: (0, 0)),
            pl.BlockSpec((1, d, tn), lambda l, j: (l, 0, j)),
            pl.BlockSpec((1, 1, tn), lambda l, j: (l, 0, j)),
        ],
        out_specs=pl.BlockSpec((1, COND_ROWS, tn), lambda l, j: (l, 0, j)),
        out_shape=jax.ShapeDtypeStruct((depth, COND_ROWS, n), F32),
        compiler_params=_params(("arbitrary", "arbitrary")),
        name="modulation",
    )(cond, w_mod, b_mod.reshape(depth, 1, n))


def _norm_mod(x, g, scale1, shift):
    return (_rms(x, D_MODEL) * g) * scale1 + shift


def _inproj_kernel(y_ref, mod_ref, g_ref, w_ref, o_ref, h_scr, *, tm, chunk):
    @pl.when(pl.program_id(1) == 0)
    def _():
        g = g_ref[...]
        shift = mod_ref[0, :, 0:D_MODEL]
        scale1 = 1.0 + mod_ref[0, :, D_MODEL:2 * D_MODEL]

        def body(r0):
            x = y_ref[pl.ds(r0, chunk), :]
            h_scr[pl.ds(r0, chunk), :] = _norm_mod(x, g, scale1, shift).astype(BF16)
        _row_loop(tm, chunk, body, unroll=4)

    o_ref[...] = jnp.dot(h_scr[...], w_ref[...], preferred_element_type=F32)


def _inproj(y, mod, g, w, rows_per_mod, tm=1024, tn=1024, chunk=32):
    rows, d = y.shape
    n = w.shape[1]
    tiles_per_mod = rows_per_mod // tm
    return pl.pallas_call(
        functools.partial(_inproj_kernel, tm=tm, chunk=chunk),
        grid=(rows // tm, n // tn),
        in_specs=[
            pl.BlockSpec((tm, d), lambda i, j: (i, 0)),
            pl.BlockSpec((1, 1, 3 * d), lambda i, j: (i // tiles_per_mod, 0, 0)),
            pl.BlockSpec((1, d), lambda i, j: (0, 0)),
            pl.BlockSpec((d, tn), lambda i, j: (0, j)),
        ],
        out_specs=pl.BlockSpec((tm, tn), lambda i, j: (i, j)),
        out_shape=jax.ShapeDtypeStruct((rows, n), F32),
        scratch_shapes=[pltpu.VMEM((tm, d), BF16)],
        compiler_params=_params(("arbitrary", "arbitrary")),
        name="attn_inproj",
    )(y, mod, g, w)


ATTN_SUB_ROWS = 256


def _rope(x, cos, sin_lo, sin_hi):
    return x * cos + pltpu.roll(x, HEAD_DIM - AXIS_DIM // 2, 1) * sin_lo + pltpu.roll(x, AXIS_DIM // 2, 1) * sin_hi


def _attn_kernel(*refs, t_len, tq, n_ctx, n_seq, latent):
    if latent:
        (q_ref, k_ref, v_ref, g_ref, ck_ref, cv_ref, qn_ref, kn_ref,
         cosq_ref, sloq_ref, shiq_ref, cosk_ref, slok_ref, shik_ref,
         og_ref, k_scr, v_scr) = refs
    else:
        (q_ref, k_ref, v_ref, g_ref, qn_ref, kn_ref,
         og_ref, ko_ref, vo_ref, k_scr, v_scr) = refs

    kv = pl.program_id(1)

    @pl.when(pl.program_id(2) == 0)
    def _():
        for s in range(n_seq):
            rows = slice(s * t_len, (s + 1) * t_len)
            kn = _rms(k_ref[rows, :], HEAD_DIM) * kn_ref[...]
            v = v_ref[rows, :]
            if latent:
                kn = _rope(kn, cosk_ref[...], slok_ref[...], shik_ref[...])
                ctx_rows = pl.ds(kv, n_ctx, stride=N_KV_HEADS)
                k_scr[s, 0:n_ctx, :] = ck_ref[ctx_rows, :].astype(BF16)
                v_scr[s, 0:n_ctx, :] = cv_ref[ctx_rows, :].astype(BF16)
            else:
                new_rows = pl.ds(s * t_len * N_KV_HEADS + kv, t_len, stride=N_KV_HEADS)
                ko_ref[new_rows, :] = kn
                vo_ref[new_rows, :] = v
            k_scr[s, n_ctx:n_ctx + t_len, :] = kn.astype(BF16)
            v_scr[s, n_ctx:n_ctx + t_len, :] = v.astype(BF16)

    qn = qn_ref[...]
    q_scale = (HEAD_DIM ** -0.5) * LOG2_E
    sub = min(ATTN_SUB_ROWS, tq)
    chains = [(s, s * tq + r0, slice(h * HEAD_DIM, (h + 1) * HEAD_DIM))
              for s in range(n_seq) for r0 in range(0, tq, sub) for h in range(Q_PER_KV)]

    def scores(s, r0, cols):
        rows = slice(r0, r0 + sub)
        qh = _rms(q_ref[rows, cols], HEAD_DIM) * qn
        if latent:
            qh = _rope(qh, cosq_ref[rows, :], sloq_ref[rows, :], shiq_ref[rows, :])
        qh = (qh * q_scale).astype(BF16)
        return lax.dot_general(qh, k_scr[s], (((1,), (1,)), ((), ())), preferred_element_type=F32)

    sc_next = scores(*chains[0])
    for i, (s, r0, cols) in enumerate(chains):
        rows = slice(r0, r0 + sub)
        sc = sc_next
        if i + 1 < len(chains):
            sc_next = scores(*chains[i + 1])
        e = jnp.exp2(sc - jnp.max(sc, axis=-1, keepdims=True))
        denom = jnp.sum(e, axis=-1, keepdims=True)
        o = jnp.dot(e.astype(BF16), v_scr[s], preferred_element_type=F32) / denom
        og_ref[rows, cols] = (o * _silu(g_ref[rows, cols])).astype(BF16)


def _attention(p, n_batch, t_len, q_norm, k_norm, cache=None, rope=None, tq=1024, n_seq=1):
    tq = min(tq, t_len)
    assert n_seq == 1 or tq == t_len
    rows = n_batch * t_len
    nq = t_len // tq
    gw = Q_PER_KV * HEAD_DIM
    k_col = ATTN_WIDTH // HEAD_DIM
    v_col = (ATTN_WIDTH + KV_WIDTH) // HEAD_DIM
    g_col = (ATTN_WIDTH + 2 * KV_WIDTH) // gw
    latent = cache is not None
    n_ctx = cache[0].shape[0] // (n_batch * N_KV_HEADS) if latent else 0

    in_specs = [
        pl.BlockSpec((n_seq * tq, gw), lambda b, kv, qi: (b * nq + qi, kv)),
        pl.BlockSpec((n_seq * t_len, HEAD_DIM), lambda b, kv, qi: (b, k_col + kv)),
        pl.BlockSpec((n_seq * t_len, HEAD_DIM), lambda b, kv, qi: (b, v_col + kv)),
        pl.BlockSpec((n_seq * tq, gw), lambda b, kv, qi: (b * nq + qi, g_col + kv)),
    ]
    args = [p, p, p, p]
    if latent:
        in_specs += [pl.BlockSpec((n_ctx * N_KV_HEADS, HEAD_DIM), lambda b, kv, qi: (b, 0))] * 2
        args += list(cache)
    in_specs += [pl.BlockSpec((1, HEAD_DIM), lambda b, kv, qi: (0, 0))] * 2
    args += [q_norm, k_norm]
    og_spec = pl.BlockSpec((n_seq * tq, gw), lambda b, kv, qi: (b * nq + qi, kv))
    og_shape = jax.ShapeDtypeStruct((rows, ATTN_WIDTH), BF16)
    if latent:
        in_specs += [pl.BlockSpec((tq, HEAD_DIM), lambda b, kv, qi: (qi, 0))] * 3
        in_specs += [pl.BlockSpec((t_len, HEAD_DIM), lambda b, kv, qi: (0, 0))] * 3
        args += list(rope) + list(rope)
        out_specs = og_spec
        out_shape = og_shape
    else:
        kv_spec = pl.BlockSpec((n_seq * t_len * N_KV_HEADS, HEAD_DIM), lambda b, kv, qi: (b, 0))
        kv_shape = jax.ShapeDtypeStruct((rows * N_KV_HEADS, HEAD_DIM), F32)
        out_specs = (og_spec, kv_spec, kv_spec)
        out_shape = (og_shape, kv_shape, kv_shape)
    return pl.pallas_call(
        functools.partial(_attn_kernel, t_len=t_len, tq=tq, n_ctx=n_ctx, n_seq=n_seq, latent=latent),
        grid=(n_batch // n_seq, N_KV_HEADS, nq),
        in_specs=in_specs,
        out_specs=out_specs,
        out_shape=out_shape,
        scratch_shapes=[pltpu.VMEM((n_seq, n_ctx + t_len, HEAD_DIM), BF16)] * 2,
        compiler_params=_params(("arbitrary", "arbitrary", "arbitrary")),
        name="attn_latent" if latent else "attn_context",
    )(*args)


def _rope_tables(n):
    rows = n // GRID_W
    row = jnp.repeat(jnp.arange(rows, dtype=F32), GRID_W)
    col = jnp.tile(jnp.arange(GRID_W, dtype=F32), rows)
    inv = ROPE_THETA ** (-jnp.arange(0, AXIS_DIM, 2, dtype=F32) / AXIS_DIM)
    ar = row[:, None] * inv
    ac = col[:, None] * inv
    ang = jnp.concatenate([ar, ar, ac, ac], axis=-1)
    cos, sin = jnp.cos(ang), jnp.sin(ang)
    first = (jnp.arange(HEAD_DIM) % AXIS_DIM) < AXIS_DIM // 2
    return cos, jnp.where(first, -sin, 0.0), jnp.where(first, 0.0, sin)


OUT_SUB_ROWS = 256


def _outproj_kernel(x_ref, w_ref, y_ref, mod_ref, g_ref, nmod_ref, ng_ref, o_ref, h_ref, *, tm, chunk):
    g = g_ref[...]
    gate = mod_ref[0, :, 2 * D_MODEL:3 * D_MODEL]
    ng = ng_ref[...]
    nshift = nmod_ref[0, :, 0:D_MODEL]
    nscale1 = 1.0 + nmod_ref[0, :, D_MODEL:2 * D_MODEL]

    def project(r0):
        return jnp.dot(x_ref[r0:r0 + OUT_SUB_ROWS, :], w_ref[...], preferred_element_type=F32)

    m_next = project(0)
    for r0 in range(0, tm, OUT_SUB_ROWS):
        m = m_next
        if r0 + OUT_SUB_ROWS < tm:
            m_next = project(r0 + OUT_SUB_ROWS)
        for c0 in range(0, OUT_SUB_ROWS, chunk):
            rows = slice(r0 + c0, r0 + c0 + chunk)
            y_new = y_ref[rows, :] + gate * (_rms(m[c0:c0 + chunk, :], D_MODEL) * g)
            o_ref[rows, :] = y_new
            h_ref[rows, :] = _norm_mod(y_new, ng, nscale1, nshift).astype(BF16)


def _outproj(x, w, y, mod, g, next_mod, next_g, rows_per_mod, tm=512, chunk=32):
    rows, d = y.shape
    k = x.shape[1]
    tiles_per_mod = rows_per_mod // tm
    mod_spec = pl.BlockSpec((1, 1, 3 * d), lambda i: (i // tiles_per_mod, 0, 0))
    row_spec = pl.BlockSpec((tm, d), lambda i: (i, 0))
    vec_spec = pl.BlockSpec((1, d), lambda i: (0, 0))
    return pl.pallas_call(
        functools.partial(_outproj_kernel, tm=tm, chunk=chunk),
        grid=(rows // tm,),
        in_specs=[
            pl.BlockSpec((tm, k), lambda i: (i, 0)),
            pl.BlockSpec((k, d), lambda i: (0, 0)),
            row_spec, mod_spec, vec_spec, mod_spec, vec_spec,
        ],
        out_specs=(row_spec, row_spec),
        out_shape=(jax.ShapeDtypeStruct((rows, d), F32), jax.ShapeDtypeStruct((rows, d), BF16)),
        compiler_params=_params(("arbitrary",)),
        name="attn_outproj",
    )(x, w, y, mod, g, next_mod, next_g)


PERM_ROWS = 32
SCAN_STEPS = 8
HALF_SLABS = LRU_BLOCK_DIM // LANES


def _log_sigmoid(x):
    return -(jnp.maximum(-x, 0.0) + jnp.log1p(jnp.exp(-jnp.abs(x))))


def _lru_kernel(*refs, n_batch, t_len, chunk, with_final):
    (h_ref, wxb_ref, wgb_ref, cw_ref, cb_ref, wa_ref, ba_ref, wx_ref, bx_ref, lam_ref, h0_ref) = refs[:11]
    if with_final:
        y_ref, fin_ref, x_scr, g_scr, a_scr, b_scr, xc_scr, w16_scr = refs[11:]
    else:
        y_ref, x_scr, g_scr, a_scr, b_scr, xc_scr, w16_scr = refs[11:]
    rows = n_batch * t_len
    c = LRU_BLOCK_DIM
    pad_lo = CONV_LEFT * n_batch
    pad_hi = CONV_RIGHT * n_batch
    step = pl.program_id(0)
    fill = step % 2
    drain = 1 - fill
    chunks_per_seq = t_len // chunk

    def project(i, part):
        lhs = h_ref[pl.ds(pl.multiple_of(i * chunk, chunk), chunk), :]
        start = (i % chunks_per_seq) * (chunk * n_batch) + i // chunks_per_seq
        w_ref = wgb_ref if part else wxb_ref
        res = jnp.dot(lhs, w_ref[...], preferred_element_type=F32)
        for s in range(HALF_SLABS):
            lanes = slice(s * LANES, (s + 1) * LANES)
            if part:
                g_scr[HALF_SLABS * fill + s, pl.ds(start, chunk, stride=n_batch), :] = res[:, lanes]
            else:
                x_scr[s, pl.ds(pad_lo + start, chunk, stride=n_batch), :] = res[:, lanes]

    @pl.when(step == 0)
    def _():
        for s in range(HALF_SLABS):
            x_scr[s, 0:pad_lo, :] = jnp.zeros((pad_lo, LANES), F32)
            x_scr[s, pad_lo + rows:pad_lo + rows + pad_hi, :] = jnp.zeros((pad_hi, LANES), F32)

        def only_project(i, carry):
            project(i, 0)
            project(i, 1)
            return carry
        lax.fori_loop(0, rows // chunk, only_project, 0)

    @pl.when(step > 0)
    def _():
        _lru_block(project, drain, cw_ref, cb_ref, wa_ref, ba_ref, wx_ref, bx_ref, lam_ref, h0_ref,
                   y_ref, fin_ref if with_final else None, x_scr, g_scr, a_scr, b_scr, xc_scr, w16_scr,
                   n_batch=n_batch, t_len=t_len, chunk=chunk)


def _lru_block(project, drain, cw_ref, cb_ref, wa_ref, ba_ref, wx_ref, bx_ref, lam_ref, h0_ref,
               y_ref, fin_ref, x_scr, g_scr, a_scr, b_scr, xc_scr, w16_scr, *, n_batch, t_len, chunk):
    rows = n_batch * t_len
    c = LRU_BLOCK_DIM

    for r0 in range(0, rows, chunk):
        for s in range(HALF_SLABS):
            lanes = slice(s * LANES, (s + 1) * LANES)
            acc = cb_ref[:, lanes]
            for j in range(CONV_WIDTH):
                acc = acc + x_scr[s, r0 + j * n_batch:r0 + j * n_batch + chunk, :] * cw_ref[j:j + 1, lanes]
            xc_scr[r0:r0 + chunk, lanes] = acc

    for d in range(2):
        w16_scr[2 * d] = (0.5 * wa_ref[d, 0]).astype(BF16)
        w16_scr[2 * d + 1] = (0.5 * wx_ref[d, 0]).astype(BF16)
    half_ba = [0.5 * ba_ref[d:d + 1, :] for d in range(2)]
    half_bx = [0.5 * bx_ref[d:d + 1, :] for d in range(2)]
    neg_log_s = [-(RG_C * _log_sigmoid(lam_ref[d:d + 1, :])) for d in range(2)]
    exp2_s = [-LOG2_E * neg_log_s[d] for d in range(2)]

    def coeffs(k, carry):
        r0 = pl.multiple_of(k * chunk, chunk)
        x = xc_scr[pl.ds(r0, chunk), :]
        x16 = x.astype(BF16)
        for d in range(2):
            zr = jnp.dot(x16, w16_scr[2 * d], preferred_element_type=F32)
            zi = jnp.dot(x16, w16_scr[2 * d + 1], preferred_element_type=F32)
            project(k, d)
            r = 0.5 * jnp.tanh(zr + half_ba[d]) + 0.5
            i = 0.5 * jnp.tanh(zi + half_bx[d]) + 0.5
            a = jnp.exp2(r * exp2_s[d])
            m2 = jnp.tanh(r * neg_log_s[d]) * (a * a + 1.0)
            mult = jnp.where(m2 > 0.0, m2 * lax.rsqrt(m2), 0.0)
            a_scr[d, pl.ds(r0, chunk), :] = a
            b_scr[d, pl.ds(r0, chunk), :] = (mult * i) * x
        return carry
    lax.fori_loop(0, rows // chunk, coeffs, 0, unroll=4)

    hf_scr = xc_scr
    hb_scr = b_scr.at[1]
    if n_batch == 16:
        blk = SCAN_STEPS * 16

        def steps(kb, carry):
            hf, hb = carry
            base_f = pl.multiple_of(kb * blk, blk)
            base_b = pl.multiple_of(rows - blk - kb * blk, blk)
            af = a_scr[0, pl.ds(base_f, blk), :]
            bf = b_scr[0, pl.ds(base_f, blk), :]
            ab = a_scr[1, pl.ds(base_b, blk), :]
            bb = b_scr[1, pl.ds(base_b, blk), :]
            hfs, hbs = [], []
            for j in range(SCAN_STEPS):
                rf = slice(16 * j, 16 * (j + 1))
                rb = slice(blk - 16 * (j + 1), blk - 16 * j)
                hf = af[rf] * hf + bf[rf]
                hb = ab[rb] * hb + bb[rb]
                hfs.append(hf)
                hbs.append(hb)
            hf_scr[pl.ds(base_f, blk), :] = jnp.concatenate(hfs, axis=0)
            hb_scr[pl.ds(base_b, blk), :] = jnp.concatenate(hbs[::-1], axis=0)
            return hf, hb
        hf, hb = lax.fori_loop(0, t_len // SCAN_STEPS, steps, (h0_ref[0], h0_ref[1]))
        if fin_ref is not None:
            fin_ref[0] = hf
            fin_ref[1] = hb
    else:
        lower = lax.broadcasted_iota(jnp.int32, (8, c), 0) < 4

        blk = SCAN_STEPS * 8

        def steps(kb, carry):
            cf, cb = carry
            base_f = pl.multiple_of(kb * blk, blk)
            base_b = pl.multiple_of(rows - blk - kb * blk, blk)
            for j in range(SCAN_STEPS):
                rf = pl.ds(base_f + 8 * j, 8)
                rb = pl.ds(base_b + 8 * (SCAN_STEPS - 1 - j), 8)
                af = a_scr[0, rf, :]
                bf = b_scr[0, rf, :]
                ab = a_scr[1, rb, :]
                bb = b_scr[1, rb, :]
                uf = af * cf + bf
                ub = ab * cb + bb
                cf = pltpu.roll(af, 4, 0) * uf + pltpu.roll(bf, 4, 0)
                cb = pltpu.roll(ab, 4, 0) * ub + pltpu.roll(bb, 4, 0)
                hf_scr[rf, :] = jnp.where(lower, uf, pltpu.roll(cf, 4, 0))
                hb_scr[rb, :] = jnp.where(lower, pltpu.roll(cb, 4, 0), ub)
            return cf, cb
        lax.fori_loop(0, rows // blk, steps, (h0_ref[0], h0_ref[1]))

    def emit(r0):
        h = hf_scr[pl.ds(r0, chunk), :] + hb_scr[pl.ds(r0, chunk), :]
        gate = jnp.concatenate(
            [g_scr[HALF_SLABS * drain + s, pl.ds(r0, chunk), :] for s in range(HALF_SLABS)], axis=-1)
        y_ref[pl.ds(r0, chunk), :] = (h * _silu(gate)).astype(BF16)
    _row_loop(rows, chunk, emit)


def _lru(h, w_in, n_batch, t_len, conv_w, conv_b, w_a, b_a, w_x, b_x, lam, h0, with_final, chunk=256):
    rows, d = h.shape
    c = LRU_BLOCK_DIM
    h_rows = h0.shape[1]
    pad = (CONV_WIDTH - 1) * n_batch
    last = LRU_BLOCKS - 1
    nxt = lambda i: jnp.minimum(i, last)
    cur = lambda i: jnp.maximum(i - 1, 0)
    in_specs = [
        pl.BlockSpec((rows, d), lambda i: (0, 0), pipeline_mode=pl.Buffered(1)),
        pl.BlockSpec((d, c), lambda i: (0, nxt(i))),
        pl.BlockSpec((d, c), lambda i: (0, LRU_BLOCKS + nxt(i))),
        pl.BlockSpec((CONV_WIDTH, c), lambda i: (0, cur(i))),
        pl.BlockSpec((1, c), lambda i: (0, cur(i))),
        pl.BlockSpec((2, 1, c, c), lambda i: (0, cur(i), 0, 0)),
        pl.BlockSpec((2, c), lambda i: (0, cur(i))),
        pl.BlockSpec((2, 1, c, c), lambda i: (0, cur(i), 0, 0)),
        pl.BlockSpec((2, c), lambda i: (0, cur(i))),
        pl.BlockSpec((2, c), lambda i: (0, cur(i))),
        pl.BlockSpec((2, h_rows, c), lambda i: (0, 0, cur(i))),
    ]
    y_spec = pl.BlockSpec((rows, c), lambda i: (0, cur(i)))
    y_shape = jax.ShapeDtypeStruct((rows, LRU_WIDTH), BF16)
    if with_final:
        out_specs = (y_spec, pl.BlockSpec((2, n_batch, c), lambda i: (0, 0, cur(i))))
        out_shape = (y_shape, jax.ShapeDtypeStruct((2, n_batch, LRU_WIDTH), F32))
    else:
        out_specs = y_spec
        out_shape = y_shape
    return pl.pallas_call(
        functools.partial(_lru_kernel, n_batch=n_batch, t_len=t_len, chunk=chunk, with_final=with_final),
        grid=(LRU_BLOCKS + 1,),
        in_specs=in_specs,
        out_specs=out_specs,
        out_shape=out_shape,
        scratch_shapes=[
            pltpu.VMEM((HALF_SLABS, rows + pad, LANES), F32),
            pltpu.VMEM((2 * HALF_SLABS, rows, LANES), F32),
            pltpu.VMEM((2, rows, c), F32),
            pltpu.VMEM((2, rows, c), F32),
            pltpu.VMEM((rows, c), F32),
            pltpu.VMEM((4, c, c), BF16),
        ],
        compiler_params=_params(("arbitrary",)),
        name="lru_mixer",
    )(h, w_in, w_in, conv_w, conv_b, w_a, b_a, w_x, b_x, lam, h0)


MXU_COLS = 256


def _outproj_perm_kernel(x_ref, w_ref, y_ref, mod_ref, g_ref, o_ref, *slab_scrs, n_batch, tt, n_mod):
    g = g_ref[...]
    sub_t = OUT_SUB_ROWS // n_batch
    perm = min(PERM_ROWS, sub_t)
    def project(blk):
        rows = slice(blk * OUT_SUB_ROWS, (blk + 1) * OUT_SUB_ROWS)
        for n0 in range(0, D_MODEL, MXU_COLS):
            m = jnp.dot(x_ref[rows, :], w_ref[:, n0:n0 + MXU_COLS], preferred_element_type=F32)
            for c in range(MXU_COLS // LANES):
                slab_scrs[blk][n0 // LANES + c] = m[:, c * LANES:(c + 1) * LANES]

    project(0)
    for blk, slab_scr in enumerate(slab_scrs):
        if blk + 1 < len(slab_scrs):
            project(blk + 1)
        t0 = blk * sub_t
        for b in range(n_batch):
            gate = mod_ref[b if n_mod > 1 else 0][:, 2 * D_MODEL:3 * D_MODEL]
            for r0 in range(0, sub_t, perm):
                m = jnp.concatenate(
                    [slab_scr[c, pl.ds(r0 * n_batch + b, perm, stride=n_batch), :] for c in range(N_SLABS)],
                    axis=-1)
                out_rows = slice(t0 + r0, t0 + r0 + perm)
                o_ref[b, out_rows, :] = y_ref[b, out_rows, :] + gate * (_rms(m, D_MODEL) * g)


def _outproj_perm(x, w, y3, mod, g, tm=512):
    n_batch, t_len, d = y3.shape
    k = x.shape[1]
    tt = tm // n_batch
    n_mod = mod.shape[0]
    return pl.pallas_call(
        functools.partial(_outproj_perm_kernel, n_batch=n_batch, tt=tt, n_mod=n_mod),
        grid=(t_len // tt,),
        in_specs=[
            pl.BlockSpec((tm, k), lambda i: (i, 0)),
            pl.BlockSpec((k, d), lambda i: (0, 0)),
            pl.BlockSpec((n_batch, tt, d), lambda i: (0, i, 0)),
            pl.BlockSpec((n_mod, 1, 3 * d), lambda i: (0, 0, 0)),
            pl.BlockSpec((1, d), lambda i: (0, 0)),
        ],
        out_specs=pl.BlockSpec((n_batch, tt, d), lambda i: (0, i, 0)),
        out_shape=jax.ShapeDtypeStruct((n_batch, t_len, d), F32),
        scratch_shapes=[pltpu.VMEM((N_SLABS, OUT_SUB_ROWS, LANES), F32)] * (tm // OUT_SUB_ROWS),
        compiler_params=_params(("arbitrary",)),
        name="lru_outproj",
    )(x, w, y3, mod, g)


def kernel(x_prompt, x_sample, c, cache_k, cache_v, state_lru, c_ctx, w_mod, b_mod, g_pre, g_post,
           w_in_attn, q_norm, k_norm, w_out_attn, w_in_lru, conv_w, conv_b,
           w_rg_a, b_rg_a, w_rg_x, b_rg_x, lru_lambda, w_out_lru):
    n_p, t_p, d = x_prompt.shape
    n_s, t_s, _ = x_sample.shape
    depth = w_mod.shape[0]
    n_ctx = cache_k.shape[2]

    cond = jnp.concatenate([c_ctx[None, :], c, jnp.zeros((COND_ROWS - 1 - n_s, d), F32)], axis=0)
    mod = _modulation(cond, w_mod, b_mod)

    y_p = x_prompt
    y_s = x_sample
    new_k, new_v, new_h = [], [], []
    for l in range(depth):
        j = l // 2
        mod_p = mod[l, 0:1].reshape(1, 1, 3 * d)
        mod_s = mod[l, 1:1 + n_s].reshape(n_s, 1, 3 * d)
        g_pre_l = g_pre[l].reshape(1, d)
        g_post_l = g_post[l].reshape(1, d)
        if l % 2 == 0:
            w_in = w_in_attn[j].astype(BF16)
            w_out = w_out_attn[j].astype(BF16)
            qn = q_norm[j].reshape(1, HEAD_DIM)
            kn = k_norm[j].reshape(1, HEAD_DIM)
            yp2 = y_p.reshape(n_p * t_p, d)
            ys2 = y_s.reshape(n_s * t_s, d)
            pp = _inproj(yp2, mod_p, g_pre_l, w_in, rows_per_mod=n_p * t_p)
            ps = _inproj(ys2, mod_s, g_pre_l, w_in, rows_per_mod=t_s)
            og_p, k_p, v_p = _attention(pp, n_p, t_p, qn, kn, n_seq=4)
            ck = cache_k[:, j].reshape(n_s * n_ctx * N_KV_HEADS, HEAD_DIM)
            cv = cache_v[:, j].reshape(n_s * n_ctx * N_KV_HEADS, HEAD_DIM)
            og_s = _attention(ps, n_s, t_s, qn, kn, cache=(ck, cv), rope=_rope_tables(t_s))
            nl = min(l + 1, depth - 1)
            nmod_p = mod[nl, 0:1].reshape(1, 1, 3 * d)
            nmod_s = mod[nl, 1:1 + n_s].reshape(n_s, 1, 3 * d)
            ng = g_pre[nl].reshape(1, d)
            y_p, h_p = _outproj(og_p, w_out, yp2, mod_p, g_post_l, nmod_p, ng, rows_per_mod=n_p * t_p)
            y_s, h_s = _outproj(og_s, w_out, ys2, mod_s, g_post_l, nmod_s, ng, rows_per_mod=t_s)
            y_p = y_p.reshape(n_p, t_p, d)
            y_s = y_s.reshape(n_s, t_s, d)
            new_k.append(k_p.reshape(n_p, t_p, N_KV_HEADS, HEAD_DIM))
            new_v.append(v_p.reshape(n_p, t_p, N_KV_HEADS, HEAD_DIM))
        else:
            w_in = w_in_lru[j].astype(BF16)
            w_out = w_out_lru[j].astype(BF16)
            cb = conv_b[j].reshape(1, LRU_WIDTH)
            lru_args = (conv_w[j], cb, w_rg_a[j], b_rg_a[j], w_rg_x[j], b_rg_x[j], lru_lambda[j])
            h0_p = jnp.zeros((2, n_p, LRU_WIDTH), F32)
            h0_s = jnp.swapaxes(state_lru[:, j], 0, 1)
            h0_s = jnp.concatenate([h0_s, h0_s], axis=1)
            yg_p, fin = _lru(h_p, w_in, n_p, t_p, *lru_args, h0_p, with_final=True)
            yg_s = _lru(h_s, w_in, n_s, t_s, *lru_args, h0_s, with_final=False)
            y_p = _outproj_perm(yg_p, w_out, y_p, mod_p, g_post_l)
            y_s = _outproj_perm(yg_s, w_out, y_s, mod_s, g_post_l)
            new_h.append(jnp.swapaxes(fin, 0, 1))
    new_cache_k = jnp.stack(new_k, axis=1)
    new_cache_v = jnp.stack(new_v, axis=1)
    new_state_lru = jnp.stack(new_h, axis=1)
    return (y_p, y_s, new_cache_k, new_cache_v, new_state_lru)
```

```python
import functools

import jax
import jax.numpy as jnp
from jax import lax
from jax.experimental import pallas as pl
from jax.experimental.pallas import tpu as pltpu

F32 = jnp.float32
BF16 = jnp.bfloat16

D_MODEL = 2048
HEAD_DIM = 128
N_HEADS = 16
N_KV_HEADS = 4
Q_PER_KV = N_HEADS // N_KV_HEADS
ATTN_WIDTH = N_HEADS * HEAD_DIM
KV_WIDTH = N_KV_HEADS * HEAD_DIM
GRID_W = 64
AXIS_DIM = HEAD_DIM // 2
ROPE_THETA = 10000.0
LRU_WIDTH = D_MODEL
LRU_BLOCKS = 8
LRU_BLOCK_DIM = LRU_WIDTH // LRU_BLOCKS
CONV_WIDTH = 4
CONV_LEFT = (CONV_WIDTH - 1) // 2
CONV_RIGHT = CONV_WIDTH - 1 - CONV_LEFT
RG_C = 8.0
EPS = 1e-6
LOG2_E = 1.4426950408889634

LANES = 128
N_SLABS = D_MODEL // LANES
COND_ROWS = 8
VMEM_LIMIT = 60 * 1024 * 1024


def _params(sem):
    return pltpu.CompilerParams(dimension_semantics=sem, vmem_limit_bytes=VMEM_LIMIT)


def _row_loop(n_rows, chunk, body, unroll=1):
    def step(i, carry):
        body(pl.multiple_of(i * chunk, chunk))
        return carry
    lax.fori_loop(0, n_rows // chunk, step, 0, unroll=unroll)


def _rms(x, width):
    return x * lax.rsqrt(jnp.sum(x * x, axis=-1, keepdims=True) * (1.0 / width) + EPS)


def _sigmoid(x):
    return 0.5 * jnp.tanh(0.5 * x) + 0.5


def _silu(x):
    return x * _sigmoid(x)


def _mod_kernel(cond_ref, w_ref, b_ref, o_ref):
    x = _silu(cond_ref[...]).astype(BF16)
    o_ref[0] = jnp.dot(x, w_ref[0].astype(BF16), preferred_element_type=F32) + b_ref[0]


def _modulation(cond, w_mod, b_mod, tn=1024):
    depth, d, n = w_mod.shape
    return pl.pallas_call(
        _mod_kernel,
        grid=(depth, n // tn),
        in_specs=[
            pl.BlockSpec((COND_ROWS, d), lambda l, j: (0, 0)),
            pl.BlockSpec((1, d, tn), lambda l, j: (l, 0, j)),
            pl.BlockSpec((1, 1, tn), lambda l, j: (l, 0, j)),
        ],
        out_specs=pl.BlockSpec((1, COND_ROWS, tn), lambda l, j: (l, 0, j)),
        out_shape=jax.ShapeDtypeStruct((depth, COND_ROWS, n), F32),
        compiler_params=_params(("arbitrary", "arbitrary")),
        name="modulation",
    )(cond, w_mod, b_mod.reshape(depth, 1, n))


def _norm_mod(x, g, scale1, shift):
    return (_rms(x, D_MODEL) * g) * scale1 + shift


CAST_ROWS = 128


def _inproj_kernel(*refs, n_cast, tm, chunk):
    y_ref, mod_ref, g_ref, w_ref = refs[:4]
    cast_src = refs[4:4 + n_cast]
    o_ref = refs[4 + n_cast]
    cast_dst = refs[5 + n_cast:5 + 2 * n_cast]
    h_scr = refs[5 + 2 * n_cast]

    @pl.when(pl.program_id(1) == 0)
    def _():
        g = g_ref[...]
        shift = mod_ref[0, :, 0:D_MODEL]
        scale1 = 1.0 + mod_ref[0, :, D_MODEL:2 * D_MODEL]

        def body(r0):
            x = y_ref[pl.ds(r0, chunk), :]
            h_scr[pl.ds(r0, chunk), :] = _norm_mod(x, g, scale1, shift).astype(BF16)
        _row_loop(tm, chunk, body, unroll=4)

    o_ref[...] = jnp.dot(h_scr[...], w_ref[...], preferred_element_type=F32)
    for src, dst in zip(cast_src, cast_dst):
        dst[...] = src[...].astype(BF16)


def _inproj(y, mod, g, w, rows_per_mod, casts=(), tm=1024, tn=1024, chunk=32):
    rows, d = y.shape
    n = w.shape[1]
    tiles_per_mod = rows_per_mod // tm
    n_j = n // tn
    n_steps = (rows // tm) * n_j
    cast_specs = []
    for m in casts:
        tiles = m.shape[0] // CAST_ROWS
        assert tiles <= n_steps
        cast_specs.append(pl.BlockSpec(
            (CAST_ROWS, m.shape[1]), lambda i, j, tiles=tiles: (jnp.minimum(i * n_j + j, tiles - 1), 0)))
    return pl.pallas_call(
        functools.partial(_inproj_kernel, n_cast=len(casts), tm=tm, chunk=chunk),
        grid=(rows // tm, n_j),
        in_specs=[
            pl.BlockSpec((tm, d), lambda i, j: (i, 0)),
            pl.BlockSpec((1, 1, 3 * d), lambda i, j: (i // tiles_per_mod, 0, 0)),
            pl.BlockSpec((1, d), lambda i, j: (0, 0)),
            pl.BlockSpec((d, tn), lambda i, j: (0, j)),
        ] + cast_specs,
        out_specs=[pl.BlockSpec((tm, tn), lambda i, j: (i, j))] + cast_specs,
        out_shape=[jax.ShapeDtypeStruct((rows, n), F32)] + [jax.ShapeDtypeStruct(m.shape, BF16) for m in casts],
        scratch_shapes=[pltpu.VMEM((tm, d), BF16)],
        compiler_params=_params(("arbitrary", "arbitrary")),
        name="attn_inproj",
    )(y, mod, g, w, *casts)


ATTN_SUB_ROWS = 256


def _rope(x, cos, sin_lo, sin_hi):
    return x * cos + pltpu.roll(x, HEAD_DIM - AXIS_DIM // 2, 1) * sin_lo + pltpu.roll(x, AXIS_DIM // 2, 1) * sin_hi


def _attn_kernel(*refs, t_len, tq, n_ctx, n_seq, latent):
    if latent:
        (q_ref, k_ref, v_ref, g_ref, ck_ref, cv_ref, qn_ref, kn_ref,
         cosq_ref, sloq_ref, shiq_ref, cosk_ref, slok_ref, shik_ref,
         og_ref, k_scr, v_scr) = refs
    else:
        (q_ref, k_ref, v_ref, g_ref, qn_ref, kn_ref,
         og_ref, ko_ref, vo_ref, k_scr, v_scr) = refs

    kv = pl.program_id(1)

    @pl.when(pl.program_id(2) == 0)
    def _():
        for s in range(n_seq):
            rows = slice(s * t_len, (s + 1) * t_len)
            kn = _rms(k_ref[rows, :], HEAD_DIM) * kn_ref[...]
            v = v_ref[rows, :]
            if latent:
                kn = _rope(kn, cosk_ref[...], slok_ref[...], shik_ref[...])
                ctx_rows = pl.ds(kv, n_ctx, stride=N_KV_HEADS)
                k_scr[s, 0:n_ctx, :] = ck_ref[ctx_rows, :].astype(BF16)
                v_scr[s, 0:n_ctx, :] = cv_ref[ctx_rows, :].astype(BF16)
            else:
                new_rows = pl.ds(s * t_len * N_KV_HEADS + kv, t_len, stride=N_KV_HEADS)
                ko_ref[new_rows, :] = kn
                vo_ref[new_rows, :] = v
            k_scr[s, n_ctx:n_ctx + t_len, :] = kn.astype(BF16)
            v_scr[s, n_ctx:n_ctx + t_len, :] = v.astype(BF16)

    qn = qn_ref[...]
    q_scale = (HEAD_DIM ** -0.5) * LOG2_E
    sub = min(ATTN_SUB_ROWS, tq)
    chains = [(s, s * tq + r0, slice(h * HEAD_DIM, (h + 1) * HEAD_DIM))
              for s in range(n_seq) for r0 in range(0, tq, sub) for h in range(Q_PER_KV)]

    def scores(s, r0, cols):
        rows = slice(r0, r0 + sub)
        qh = _rms(q_ref[rows, cols], HEAD_DIM) * qn
        if latent:
            qh = _rope(qh, cosq_ref[rows, :], sloq_ref[rows, :], shiq_ref[rows, :])
        qh = (qh * q_scale).astype(BF16)
        return lax.dot_general(qh, k_scr[s], (((1,), (1,)), ((), ())), preferred_element_type=F32)

    sc_next = scores(*chains[0])
    for i, (s, r0, cols) in enumerate(chains):
        rows = slice(r0, r0 + sub)
        sc = sc_next
        if i + 1 < len(chains):
            sc_next = scores(*chains[i + 1])
        e = jnp.exp2(sc - jnp.max(sc, axis=-1, keepdims=True))
        denom = jnp.sum(e, axis=-1, keepdims=True)
        o = jnp.dot(e.astype(BF16), v_scr[s], preferred_element_type=F32) / denom
        og_ref[rows, cols] = (o * _silu(g_ref[rows, cols])).astype(BF16)


def _attention(p, n_batch, t_len, q_norm, k_norm, cache=None, rope=None, tq=1024, n_seq=1):
    tq = min(tq, t_len)
    assert n_seq == 1 or tq == t_len
    rows = n_batch * t_len
    nq = t_len // tq
    gw = Q_PER_KV * HEAD_DIM
    k_col = ATTN_WIDTH // HEAD_DIM
    v_col = (ATTN_WIDTH + KV_WIDTH) // HEAD_DIM
    g_col = (ATTN_WIDTH + 2 * KV_WIDTH) // gw
    latent = cache is not None
    n_ctx = cache[0].shape[0] // (n_batch * N_KV_HEADS) if latent else 0

    in_specs = [
        pl.BlockSpec((n_seq * tq, gw), lambda b, kv, qi: (b * nq + qi, kv)),
        pl.BlockSpec((n_seq * t_len, HEAD_DIM), lambda b, kv, qi: (b, k_col + kv)),
        pl.BlockSpec((n_seq * t_len, HEAD_DIM), lambda b, kv, qi: (b, v_col + kv)),
        pl.BlockSpec((n_seq * tq, gw), lambda b, kv, qi: (b * nq + qi, g_col + kv)),
    ]
    args = [p, p, p, p]
    if latent:
        in_specs += [pl.BlockSpec((n_ctx * N_KV_HEADS, HEAD_DIM), lambda b, kv, qi: (b, 0))] * 2
        args += list(cache)
    in_specs += [pl.BlockSpec((1, HEAD_DIM), lambda b, kv, qi: (0, 0))] * 2
    args += [q_norm, k_norm]
    og_spec = pl.BlockSpec((n_seq * tq, gw), lambda b, kv, qi: (b * nq + qi, kv))
    og_shape = jax.ShapeDtypeStruct((rows, ATTN_WIDTH), BF16)
    if latent:
        in_specs += [pl.BlockSpec((tq, HEAD_DIM), lambda b, kv, qi: (qi, 0))] * 3
        in_specs += [pl.BlockSpec((t_len, HEAD_DIM), lambda b, kv, qi: (0, 0))] * 3
        args += list(rope) + list(rope)
        out_specs = og_spec
        out_shape = og_shape
    else:
        kv_spec = pl.BlockSpec((n_seq * t_len * N_KV_HEADS, HEAD_DIM), lambda b, kv, qi: (b, 0))
        kv_shape = jax.ShapeDtypeStruct((rows * N_KV_HEADS, HEAD_DIM), F32)
        out_specs = (og_spec, kv_spec, kv_spec)
        out_shape = (og_shape, kv_shape, kv_shape)
    return pl.pallas_call(
        functools.partial(_attn_kernel, t_len=t_len, tq=tq, n_ctx=n_ctx, n_seq=n_seq, latent=latent),
        grid=(n_batch // n_seq, N_KV_HEADS, nq),
        in_specs=in_specs,
        out_specs=out_specs,
        out_shape=out_shape,
        scratch_shapes=[pltpu.VMEM((n_seq, n_ctx + t_len, HEAD_DIM), BF16)] * 2,
        compiler_params=_params(("arbitrary", "arbitrary", "arbitrary")),
        name="attn_latent" if latent else "attn_context",
    )(*args)


def _rope_tables(n):
    rows = n // GRID_W
    row = jnp.repeat(jnp.arange(rows, dtype=F32), GRID_W)
    col = jnp.tile(jnp.arange(GRID_W, dtype=F32), rows)
    inv = ROPE_THETA ** (-jnp.arange(0, AXIS_DIM, 2, dtype=F32) / AXIS_DIM)
    ar = row[:, None] * inv
    ac = col[:, None] * inv
    ang = jnp.concatenate([ar, ar, ac, ac], axis=-1)
    cos, sin = jnp.cos(ang), jnp.sin(ang)
    first = (jnp.arange(HEAD_DIM) % AXIS_DIM) < AXIS_DIM // 2
    return cos, jnp.where(first, -sin, 0.0), jnp.where(first, 0.0, sin)


OUT_SUB_ROWS = 256


def _outproj_kernel(x_ref, w_ref, y_ref, mod_ref, g_ref, nmod_ref, ng_ref, o_ref, h_ref, *, tm, chunk):
    g = g_ref[...]
    gate = mod_ref[0, :, 2 * D_MODEL:3 * D_MODEL]
    ng = ng_ref[...]
    nshift = nmod_ref[0, :, 0:D_MODEL]
    nscale1 = 1.0 + nmod_ref[0, :, D_MODEL:2 * D_MODEL]

    def project(r0):
        return jnp.dot(x_ref[r0:r0 + OUT_SUB_ROWS, :], w_ref[...], preferred_element_type=F32)

    m_next = project(0)
    for r0 in range(0, tm, OUT_SUB_ROWS):
        m = m_next
        if r0 + OUT_SUB_ROWS < tm:
            m_next = project(r0 + OUT_SUB_ROWS)
        for c0 in range(0, OUT_SUB_ROWS, chunk):
            rows = slice(r0 + c0, r0 + c0 + chunk)
            y_new = y_ref[rows, :] + gate * (_rms(m[c0:c0 + chunk, :], D_MODEL) * g)
            o_ref[rows, :] = y_new
            h_ref[rows, :] = _norm_mod(y_new, ng, nscale1, nshift).astype(BF16)


def _outproj(x, w, y, mod, g, next_mod, next_g, rows_per_mod, tm=512, chunk=32):
    rows, d = y.shape
    k = x.shape[1]
    tiles_per_mod = rows_per_mod // tm
    mod_spec = pl.BlockSpec((1, 1, 3 * d), lambda i: (i // tiles_per_mod, 0, 0))
    row_spec = pl.BlockSpec((tm, d), lambda i: (i, 0))
    vec_spec = pl.BlockSpec((1, d), lambda i: (0, 0))
    return pl.pallas_call(
        functools.partial(_outproj_kernel, tm=tm, chunk=chunk),
        grid=(rows // tm,),
        in_specs=[
            pl.BlockSpec((tm, k), lambda i: (i, 0)),
            pl.BlockSpec((k, d), lambda i: (0, 0)),
            row_spec, mod_spec, vec_spec, mod_spec, vec_spec,
        ],
        out_specs=(row_spec, row_spec),
        out_shape=(jax.ShapeDtypeStruct((rows, d), F32), jax.ShapeDtypeStruct((rows, d), BF16)),
        compiler_params=_params(("arbitrary",)),
        name="attn_outproj",
    )(x, w, y, mod, g, next_mod, next_g)


PERM_ROWS = 32
SCAN_STEPS = 8
PROJ_ROWS = 256
PROJ_K_SPLITS = 1
HALF_SLABS = LRU_BLOCK_DIM // LANES


def _log_sigmoid(x):
    return -(jnp.maximum(-x, 0.0) + jnp.log1p(jnp.exp(-jnp.abs(x))))


def _lru_kernel(*refs, n_batch, t_len, chunk, with_final):
    (h_ref, wxb_ref, wgb_ref, cw_ref, cb_ref, wa_ref, ba_ref, wx_ref, bx_ref, lam_ref, h0_ref) = refs[:11]
    if with_final:
        y_ref, fin_ref, x_scr, g_scr, a_scr, b_scr, xc_scr, w16_scr = refs[11:]
    else:
        y_ref, x_scr, g_scr, a_scr, b_scr, xc_scr, w16_scr = refs[11:]
    rows = n_batch * t_len
    c = LRU_BLOCK_DIM
    pad_lo = CONV_LEFT * n_batch
    pad_hi = CONV_RIGHT * n_batch
    step = pl.program_id(0)
    fill = step % 2
    drain = 1 - fill
    seg = min(PROJ_ROWS, t_len)
    k_piece = h_ref.shape[1] // PROJ_K_SPLITS

    def project_piece(i, part, split):
        w_ref = wgb_ref if part else wxb_ref
        cols = slice(split * k_piece, (split + 1) * k_piece)
        lhs = h_ref[pl.ds(pl.multiple_of(i * PROJ_ROWS, PROJ_ROWS), PROJ_ROWS), cols]
        return jnp.dot(lhs, w_ref[cols, :], preferred_element_type=F32)

    def project_store(i, part, res):
        for q in range(PROJ_ROWS // seg):
            first = i * PROJ_ROWS + q * seg
            start = (first % t_len) * n_batch + first // t_len
            for s in range(HALF_SLABS):
                val = res[q * seg:(q + 1) * seg, s * LANES:(s + 1) * LANES]
                if part:
                    g_scr[HALF_SLABS * fill + s, pl.ds(start, seg, stride=n_batch), :] = val
                else:
                    x_scr[s, pl.ds(pad_lo + start, seg, stride=n_batch), :] = val

    @pl.when(step == 0)
    def _():
        for s in range(HALF_SLABS):
            x_scr[s, 0:pad_lo, :] = jnp.zeros((pad_lo, LANES), F32)
            x_scr[s, pad_lo + rows:pad_lo + rows + pad_hi, :] = jnp.zeros((pad_hi, LANES), F32)

        def only_project(i, carry):
            for part in range(2):
                res = project_piece(i, part, 0)
                for split in range(1, PROJ_K_SPLITS):
                    res = res + project_piece(i, part, split)
                project_store(i, part, res)
            return carry
        lax.fori_loop(0, rows // PROJ_ROWS, only_project, 0)

    @pl.when(step > 0)
    def _():
        _lru_block(project_piece, project_store, drain, cw_ref, cb_ref, wa_ref, ba_ref, wx_ref, bx_ref, lam_ref, h0_ref,
                   y_ref, fin_ref if with_final else None, x_scr, g_scr, a_scr, b_scr, xc_scr, w16_scr,
                   n_batch=n_batch, t_len=t_len, chunk=chunk)


def _lru_block(project_piece, project_store, drain, cw_ref, cb_ref, wa_ref, ba_ref, wx_ref, bx_ref, lam_ref, h0_ref,
               y_ref, fin_ref, x_scr, g_scr, a_scr, b_scr, xc_scr, w16_scr, *, n_batch, t_len, chunk):
    rows = n_batch * t_len
    c = LRU_BLOCK_DIM

    for r0 in range(0, rows, chunk):
        for s in range(HALF_SLABS):
            lanes = slice(s * LANES, (s + 1) * LANES)
            acc = cb_ref[:, lanes]
            for j in range(CONV_WIDTH):
                acc = acc + x_scr[s, r0 + j * n_batch:r0 + j * n_batch + chunk, :] * cw_ref[j:j + 1, lanes]
            xc_scr[r0:r0 + chunk, lanes] = acc

    for d in range(2):
        w16_scr[2 * d] = (0.5 * wa_ref[d, 0]).astype(BF16)
        w16_scr[2 * d + 1] = (0.5 * wx_ref[d, 0]).astype(BF16)
    half_ba = [0.5 * ba_ref[d:d + 1, :] for d in range(2)]
    half_bx = [0.5 * bx_ref[d:d + 1, :] for d in range(2)]
    neg_log_s = [-(RG_C * _log_sigmoid(lam_ref[d:d + 1, :])) for d in range(2)]
    exp2_s = [-LOG2_E * neg_log_s[d] for d in range(2)]

    pieces = [(part, split) for part in range(2) for split in range(PROJ_K_SPLITS)]
    assert PROJ_ROWS // chunk * 2 == len(pieces)

    def coeffs(k, carry):
        partial = None
        for sub in range(PROJ_ROWS // chunk):
            r0 = pl.multiple_of(k * PROJ_ROWS + sub * chunk, chunk)
            x = xc_scr[pl.ds(r0, chunk), :]
            x16 = x.astype(BF16)
            for d in range(2):
                zr = jnp.dot(x16, w16_scr[2 * d], preferred_element_type=F32)
                zi = jnp.dot(x16, w16_scr[2 * d + 1], preferred_element_type=F32)
                part, split = pieces[2 * sub + d]
                piece = project_piece(k, part, split)
                partial = piece if split == 0 else partial + piece
                if split == PROJ_K_SPLITS - 1:
                    project_store(k, part, partial)
                r = 0.5 * jnp.tanh(zr + half_ba[d]) + 0.5
                i = 0.5 * jnp.tanh(zi + half_bx[d]) + 0.5
                a = jnp.exp2(r * exp2_s[d])
                m2 = jnp.tanh(r * neg_log_s[d]) * (a * a + 1.0)
                mult = jnp.where(m2 > 0.0, m2 * lax.rsqrt(m2), 0.0)
                a_scr[d, pl.ds(r0, chunk), :] = a
                b_scr[d, pl.ds(r0, chunk), :] = (mult * i) * x
        return carry
    lax.fori_loop(0, rows // PROJ_ROWS, coeffs, 0, unroll=4)

    hf_scr = xc_scr
    hb_scr = b_scr.at[1]
    if n_batch == 16:
        blk = SCAN_STEPS * 16

        def steps(kb, carry):
            hf, hb = carry
            base_f = pl.multiple_of(kb * blk, blk)
            base_b = pl.multiple_of(rows - blk - kb * blk, blk)
            af = a_scr[0, pl.ds(base_f, blk), :]
            bf = b_scr[0, pl.ds(base_f, blk), :]
            ab = a_scr[1, pl.ds(base_b, blk), :]
            bb = b_scr[1, pl.ds(base_b, blk), :]
            hfs, hbs = [], []
            for j in range(SCAN_STEPS):
                rf = slice(16 * j, 16 * (j + 1))
                rb = slice(blk - 16 * (j + 1), blk - 16 * j)
                hf = af[rf] * hf + bf[rf]
                hb = ab[rb] * hb + bb[rb]
                hfs.append(hf)
                hbs.append(hb)
            hf_scr[pl.ds(base_f, blk), :] = jnp.concatenate(hfs, axis=0)
            hb_scr[pl.ds(base_b, blk), :] = jnp.concatenate(hbs[::-1], axis=0)
            return hf, hb
        hf, hb = lax.fori_loop(0, t_len // SCAN_STEPS, steps, (h0_ref[0], h0_ref[1]))
        if fin_ref is not None:
            fin_ref[0] = hf
            fin_ref[1] = hb
    else:
        lower = lax.broadcasted_iota(jnp.int32, (8, c), 0) < 4

        blk = SCAN_STEPS * 8

        def steps(kb, carry):
            cf, cb = carry
            base_f = pl.multiple_of(kb * blk, blk)
            base_b = pl.multiple_of(rows - blk - kb * blk, blk)
            a_f = a_scr[0, pl.ds(base_f, blk), :]
            b_f = b_scr[0, pl.ds(base_f, blk), :]
            a_b = a_scr[1, pl.ds(base_b, blk), :]
            b_b = b_scr[1, pl.ds(base_b, blk), :]
            hfs, hbs = [], []
            for j in range(SCAN_STEPS):
                rf = slice(8 * j, 8 * (j + 1))
                rb = slice(blk - 8 * (j + 1), blk - 8 * j)
                af, bf, ab, bb = a_f[rf], b_f[rf], a_b[rb], b_b[rb]
                uf = af * cf + bf
                ub = ab * cb + bb
                cf = pltpu.roll(af, 4, 0) * uf + pltpu.roll(bf, 4, 0)
                cb = pltpu.roll(ab, 4, 0) * ub + pltpu.roll(bb, 4, 0)
                hfs.append(jnp.where(lower, uf, pltpu.roll(cf, 4, 0)))
                hbs.append(jnp.where(lower, pltpu.roll(cb, 4, 0), ub))
            hf_scr[pl.ds(base_f, blk), :] = jnp.concatenate(hfs, axis=0)
            hb_scr[pl.ds(base_b, blk), :] = jnp.concatenate(hbs[::-1], axis=0)
            return cf, cb
        lax.fori_loop(0, rows // blk, steps, (h0_ref[0], h0_ref[1]))

    def emit(r0):
        h = hf_scr[pl.ds(r0, chunk), :] + hb_scr[pl.ds(r0, chunk), :]
        gate = jnp.concatenate(
            [g_scr[HALF_SLABS * drain + s, pl.ds(r0, chunk), :] for s in range(HALF_SLABS)], axis=-1)
        y_ref[pl.ds(r0, chunk), :] = (h * _silu(gate)).astype(BF16)
    _row_loop(rows, chunk, emit)


def _lru(h, w_in, n_batch, t_len, conv_w, conv_b, w_a, b_a, w_x, b_x, lam, h0, with_final, chunk=256):
    rows, d = h.shape
    c = LRU_BLOCK_DIM
    h_rows = h0.shape[1]
    pad = (CONV_WIDTH - 1) * n_batch
    last = LRU_BLOCKS - 1
    nxt = lambda i: jnp.minimum(i, last)
    cur = lambda i: jnp.maximum(i - 1, 0)
    in_specs = [
        pl.BlockSpec((rows, d), lambda i: (0, 0), pipeline_mode=pl.Buffered(1)),
        pl.BlockSpec((d, c), lambda i: (0, nxt(i))),
        pl.BlockSpec((d, c), lambda i: (0, LRU_BLOCKS + nxt(i))),
        pl.BlockSpec((CONV_WIDTH, c), lambda i: (0, cur(i))),
        pl.BlockSpec((1, c), lambda i: (0, cur(i))),
        pl.BlockSpec((2, 1, c, c), lambda i: (0, cur(i), 0, 0)),
        pl.BlockSpec((2, c), lambda i: (0, cur(i))),
        pl.BlockSpec((2, 1, c, c), lambda i: (0, cur(i), 0, 0)),
        pl.BlockSpec((2, c), lambda i: (0, cur(i))),
        pl.BlockSpec((2, c), lambda i: (0, cur(i))),
        pl.BlockSpec((2, h_rows, c), lambda i: (0, 0, cur(i))),
    ]
    y_spec = pl.BlockSpec((rows, c), lambda i: (0, cur(i)))
    y_shape = jax.ShapeDtypeStruct((rows, LRU_WIDTH), BF16)
    if with_final:
        out_specs = (y_spec, pl.BlockSpec((2, n_batch, c), lambda i: (0, 0, cur(i))))
        out_shape = (y_shape, jax.ShapeDtypeStruct((2, n_batch, LRU_WIDTH), F32))
    else:
        out_specs = y_spec
        out_shape = y_shape
    return pl.pallas_call(
        functools.partial(_lru_kernel, n_batch=n_batch, t_len=t_len, chunk=chunk, with_final=with_final),
        grid=(LRU_BLOCKS + 1,),
        in_specs=in_specs,
        out_specs=out_specs,
        out_shape=out_shape,
        scratch_shapes=[
            pltpu.VMEM((HALF_SLABS, rows + pad, LANES), F32),
            pltpu.VMEM((2 * HALF_SLABS, rows, LANES), F32),
            pltpu.VMEM((2, rows, c), F32),
            pltpu.VMEM((2, rows, c), F32),
            pltpu.VMEM((rows, c), F32),
            pltpu.VMEM((4, c, c), BF16),
        ],
        compiler_params=_params(("arbitrary",)),
        name="lru_mixer",
    )(h, w_in, w_in, conv_w, conv_b, w_a, b_a, w_x, b_x, lam, h0)


MXU_COLS = 256


def _outproj_perm_kernel(x_ref, w_ref, y_ref, mod_ref, g_ref, o_ref, *slab_scrs, n_batch, tt, n_mod):
    g = g_ref[...]
    sub_t = OUT_SUB_ROWS // n_batch
    perm = min(PERM_ROWS, sub_t)
    def project(blk):
        rows = slice(blk * OUT_SUB_ROWS, (blk + 1) * OUT_SUB_ROWS)
        for n0 in range(0, D_MODEL, MXU_COLS):
            m = jnp.dot(x_ref[rows, :], w_ref[:, n0:n0 + MXU_COLS], preferred_element_type=F32)
            for c in range(MXU_COLS // LANES):
                slab_scrs[blk][n0 // LANES + c] = m[:, c * LANES:(c + 1) * LANES]

    project(0)
    for blk, slab_scr in enumerate(slab_scrs):
        if blk + 1 < len(slab_scrs):
            project(blk + 1)
        t0 = blk * sub_t
        for b in range(n_batch):
            gate = mod_ref[b if n_mod > 1 else 0][:, 2 * D_MODEL:3 * D_MODEL]
            for r0 in range(0, sub_t, perm):
                m = jnp.concatenate(
                    [slab_scr[c, pl.ds(r0 * n_batch + b, perm, stride=n_batch), :] for c in range(N_SLABS)],
                    axis=-1)
                out_rows = slice(t0 + r0, t0 + r0 + perm)
                o_ref[b, out_rows, :] = y_ref[b, out_rows, :] + gate * (_rms(m, D_MODEL) * g)


def _outproj_perm(x, w, y3, mod, g, tm=512):
    n_batch, t_len, d = y3.shape
    k = x.shape[1]
    tt = tm // n_batch
    n_mod = mod.shape[0]
    return pl.pallas_call(
        functools.partial(_outproj_perm_kernel, n_batch=n_batch, tt=tt, n_mod=n_mod),
        grid=(t_len // tt,),
        in_specs=[
            pl.BlockSpec((tm, k), lambda i: (i, 0)),
            pl.BlockSpec((k, d), lambda i: (0, 0)),
            pl.BlockSpec((n_batch, tt, d), lambda i: (0, i, 0)),
            pl.BlockSpec((n_mod, 1, 3 * d), lambda i: (0, 0, 0)),
            pl.BlockSpec((1, d), lambda i: (0, 0)),
        ],
        out_specs=pl.BlockSpec((n_batch, tt, d), lambda i: (0, i, 0)),
        out_shape=jax.ShapeDtypeStruct((n_batch, t_len, d), F32),
        scratch_shapes=[pltpu.VMEM((N_SLABS, OUT_SUB_ROWS, LANES), F32)] * (tm // OUT_SUB_ROWS),
        compiler_params=_params(("arbitrary",)),
        name="lru_outproj",
    )(x, w, y3, mod, g)


def kernel(x_prompt, x_sample, c, cache_k, cache_v, state_lru, c_ctx, w_mod, b_mod, g_pre, g_post,
           w_in_attn, q_norm, k_norm, w_out_attn, w_in_lru, conv_w, conv_b,
           w_rg_a, b_rg_a, w_rg_x, b_rg_x, lru_lambda, w_out_lru):
    n_p, t_p, d = x_prompt.shape
    n_s, t_s, _ = x_sample.shape
    depth = w_mod.shape[0]
    n_ctx = cache_k.shape[2]

    cond = jnp.concatenate([c_ctx[None, :], c, jnp.zeros((COND_ROWS - 1 - n_s, d), F32)], axis=0)
    mod = _modulation(cond, w_mod, b_mod)

    y_p = x_prompt
    y_s = x_sample
    new_k, new_v, new_h = [], [], []
    for l in range(depth):
        j = l // 2
        mod_p = mod[l, 0:1].reshape(1, 1, 3 * d)
        mod_s = mod[l, 1:1 + n_s].reshape(n_s, 1, 3 * d)
        g_pre_l = g_pre[l].reshape(1, d)
        g_post_l = g_post[l].reshape(1, d)
        if l % 2 == 0:
            w_in = w_in_attn[j].astype(BF16)
            qn = q_norm[j].reshape(1, HEAD_DIM)
            kn = k_norm[j].reshape(1, HEAD_DIM)
            yp2 = y_p.reshape(n_p * t_p, d)
            ys2 = y_s.reshape(n_s * t_s, d)
            has_lru = l + 1 < depth
            later_p = (w_out_attn[j], w_out_lru[j]) if has_lru else (w_out_attn[j],)
            later_s = (w_in_lru[j],) if has_lru else ()
            pp, w_out, *rest_p = _inproj(yp2, mod_p, g_pre_l, w_in, rows_per_mod=n_p * t_p, casts=later_p)
            ps, *rest_s = _inproj(ys2, mod_s, g_pre_l, w_in, rows_per_mod=t_s, casts=later_s)
            if has_lru:
                lru_w_out, lru_w_in = rest_p[0], rest_s[0]
            og_p, k_p, v_p = _attention(pp, n_p, t_p, qn, kn, n_seq=4)
            ck = cache_k[:, j].reshape(n_s * n_ctx * N_KV_HEADS, HEAD_DIM)
            cv = cache_v[:, j].reshape(n_s * n_ctx * N_KV_HEADS, HEAD_DIM)
            og_s = _attention(ps, n_s, t_s, qn, kn, cache=(ck, cv), rope=_rope_tables(t_s))
            nl = min(l + 1, depth - 1)
            nmod_p = mod[nl, 0:1].reshape(1, 1, 3 * d)
            nmod_s = mod[nl, 1:1 + n_s].reshape(n_s, 1, 3 * d)
            ng = g_pre[nl].reshape(1, d)
            y_p, h_p = _outproj(og_p, w_out, yp2, mod_p, g_post_l, nmod_p, ng, rows_per_mod=n_p * t_p)
            y_s, h_s = _outproj(og_s, w_out, ys2, mod_s, g_post_l, nmod_s, ng, rows_per_mod=t_s)
            y_p = y_p.reshape(n_p, t_p, d)
            y_s = y_s.reshape(n_s, t_s, d)
            new_k.append(k_p.reshape(n_p, t_p, N_KV_HEADS, HEAD_DIM))
            new_v.append(v_p.reshape(n_p, t_p, N_KV_HEADS, HEAD_DIM))
        else:
            w_in, w_out = lru_w_in, lru_w_out
            cb = conv_b[j].reshape(1, LRU_WIDTH)
            lru_args = (conv_w[j], cb, w_rg_a[j], b_rg_a[j], w_rg_x[j], b_rg_x[j], lru_lambda[j])
            h0_p = jnp.zeros((2, n_p, LRU_WIDTH), F32)
            h0_s = jnp.swapaxes(state_lru[:, j], 0, 1)
            h0_s = jnp.concatenate([h0_s, h0_s], axis=1)
            yg_p, fin = _lru(h_p, w_in, n_p, t_p, *lru_args, h0_p, with_final=True)
            yg_s = _lru(h_s, w_in, n_s, t_s, *lru_args, h0_s, with_final=False)
            y_p = _outproj_perm(yg_p, w_out, y_p, mod_p, g_post_l)
            y_s = _outproj_perm(yg_s, w_out, y_s, mod_s, g_post_l)
            new_h.append(jnp.swapaxes(fin, 0, 1))
    new_cache_k = jnp.stack(new_k, axis=1)
    new_cache_v = jnp.stack(new_v, axis=1)
    new_state_lru = jnp.stack(new_h, axis=1)
    return (y_p, y_s, new_cache_k, new_cache_v, new_state_lru)
```

```python
import functools

import jax
import jax.numpy as jnp
from jax import lax
from jax.experimental import pallas as pl
from jax.experimental.pallas import tpu as pltpu

F32 = jnp.float32
BF16 = jnp.bfloat16

D_MODEL = 2048
HEAD_DIM = 128
N_HEADS = 16
N_KV_HEADS = 4
Q_PER_KV = N_HEADS // N_KV_HEADS
ATTN_WIDTH = N_HEADS * HEAD_DIM
KV_WIDTH = N_KV_HEADS * HEAD_DIM
GRID_W = 64
AXIS_DIM = HEAD_DIM // 2
ROPE_THETA = 10000.0
LRU_WIDTH = D_MODEL
LRU_BLOCKS = 8
LRU_BLOCK_DIM = LRU_WIDTH // LRU_BLOCKS
CONV_WIDTH = 4
CONV_LEFT = (CONV_WIDTH - 1) // 2
CONV_RIGHT = CONV_WIDTH - 1 - CONV_LEFT
RG_C = 8.0
EPS = 1e-6
LOG2_E = 1.4426950408889634

LANES = 128
N_SLABS = D_MODEL // LANES
COND_ROWS = 8
VMEM_LIMIT = 60 * 1024 * 1024


def _params(sem):
    return pltpu.CompilerParams(dimension_semantics=sem, vmem_limit_bytes=VMEM_LIMIT)


def _row_loop(n_rows, chunk, body, unroll=1):
    def step(i, carry):
        body(pl.multiple_of(i * chunk, chunk))
        return carry
    lax.fori_loop(0, n_rows // chunk, step, 0, unroll=unroll)


def _rms(x, width):
    return x * lax.rsqrt(jnp.sum(x * x, axis=-1, keepdims=True) * (1.0 / width) + EPS)


def _sigmoid(x):
    return 0.5 * jnp.tanh(0.5 * x) + 0.5


def _silu(x):
    return x * _sigmoid(x)


def _mod_kernel(cond_ref, w_ref, b_ref, o_ref):
    x = _silu(cond_ref[...]).astype(BF16)
    o_ref[0] = jnp.dot(x, w_ref[0].astype(BF16), preferred_element_type=F32) + b_ref[0]


def _modulation(cond, w_mod, b_mod, tn=1024):
    depth, d, n = w_mod.shape
    return pl.pallas_call(
        _mod_kernel,
        grid=(depth, n // tn),
        in_specs=[
            pl.BlockSpec((COND_ROWS, d), lambda l, j: (0, 0)),
            pl.BlockSpec((1, d, tn), lambda l, j: (l, 0, j)),
            pl.BlockSpec((1, 1, tn), lambda l, j: (l, 0, j)),
        ],
        out_specs=pl.BlockSpec((1, COND_ROWS, tn), lambda l, j: (l, 0, j)),
        out_shape=jax.ShapeDtypeStruct((depth, COND_ROWS, n), F32),
        compiler_params=_params(("arbitrary", "arbitrary")),
        name="modulation",
    )(cond, w_mod, b_mod.reshape(depth, 1, n))


def _norm_mod(x, g, scale1, shift):
    return (_rms(x, D_MODEL) * g) * scale1 + shift


CAST_ROWS = 128


def _cast_specs(casts, grid, step_of):
    n_steps = 1
    for extent in grid:
        n_steps *= extent
    specs = []
    for m in casts:
        tiles = m.shape[0] // CAST_ROWS
        assert tiles * CAST_ROWS == m.shape[0] and tiles <= n_steps
        specs.append(pl.BlockSpec(
            (CAST_ROWS, m.shape[1]), lambda *idx, tiles=tiles: (jnp.minimum(step_of(*idx), tiles - 1), 0)))
    return specs


def _inproj_kernel(*refs, n_cast, tm, chunk):
    y_ref, mod_ref, g_ref, w_ref = refs[:4]
    cast_src = refs[4:4 + n_cast]
    o_ref = refs[4 + n_cast]
    cast_dst = refs[5 + n_cast:5 + 2 * n_cast]
    h_scr = refs[5 + 2 * n_cast]

    @pl.when(pl.program_id(1) == 0)
    def _():
        g = g_ref[...]
        shift = mod_ref[0, :, 0:D_MODEL]
        scale1 = 1.0 + mod_ref[0, :, D_MODEL:2 * D_MODEL]

        def body(r0):
            x = y_ref[pl.ds(r0, chunk), :]
            h_scr[pl.ds(r0, chunk), :] = _norm_mod(x, g, scale1, shift).astype(BF16)
        _row_loop(tm, chunk, body, unroll=4)

    o_ref[...] = jnp.dot(h_scr[...], w_ref[...], preferred_element_type=F32)
    for src, dst in zip(cast_src, cast_dst):
        dst[...] = src[...].astype(BF16)


def _inproj(y, mod, g, w, rows_per_mod, casts=(), tm=1024, tn=1024, chunk=32):
    rows, d = y.shape
    n = w.shape[1]
    tiles_per_mod = rows_per_mod // tm
    n_j = n // tn
    grid = (rows // tm, n_j)
    cast_specs = _cast_specs(casts, grid, lambda i, j: i * n_j + j)
    return pl.pallas_call(
        functools.partial(_inproj_kernel, n_cast=len(casts), tm=tm, chunk=chunk),
        grid=grid,
        in_specs=[
            pl.BlockSpec((tm, d), lambda i, j: (i, 0)),
            pl.BlockSpec((1, 1, 3 * d), lambda i, j: (i // tiles_per_mod, 0, 0)),
            pl.BlockSpec((1, d), lambda i, j: (0, 0)),
            pl.BlockSpec((d, tn), lambda i, j: (0, j)),
        ] + cast_specs,
        out_specs=[pl.BlockSpec((tm, tn), lambda i, j: (i, j))] + cast_specs,
        out_shape=[jax.ShapeDtypeStruct((rows, n), F32)] + [jax.ShapeDtypeStruct(m.shape, BF16) for m in casts],
        scratch_shapes=[pltpu.VMEM((tm, d), BF16)],
        compiler_params=_params(("arbitrary", "arbitrary")),
        name="attn_inproj",
    )(y, mod, g, w, *casts)


ATTN_SUB_ROWS = 256


def _rope(x, cos, sin_lo, sin_hi):
    return x * cos + pltpu.roll(x, HEAD_DIM - AXIS_DIM // 2, 1) * sin_lo + pltpu.roll(x, AXIS_DIM // 2, 1) * sin_hi


def _attn_kernel(*refs, t_len, tq, n_ctx, n_seq, n_cast, latent):
    n_in = 14 if latent else 6
    cast_src = refs[n_in:n_in + n_cast]
    n_out = 1 if latent else 3
    cast_dst = refs[n_in + n_cast + n_out:n_in + 2 * n_cast + n_out]
    k_scr, v_scr = refs[-2:]
    if latent:
        (q_ref, k_ref, v_ref, g_ref, ck_ref, cv_ref, qn_ref, kn_ref,
         cosq_ref, sloq_ref, shiq_ref, cosk_ref, slok_ref, shik_ref) = refs[:n_in]
        og_ref = refs[n_in + n_cast]
    else:
        q_ref, k_ref, v_ref, g_ref, qn_ref, kn_ref = refs[:n_in]
        og_ref, ko_ref, vo_ref = refs[n_in + n_cast:n_in + n_cast + n_out]
    for src, dst in zip(cast_src, cast_dst):
        dst[...] = src[...].astype(BF16)

    kv = pl.program_id(1)

    @pl.when(pl.program_id(2) == 0)
    def _():
        for s in range(n_seq):
            rows = slice(s * t_len, (s + 1) * t_len)
            kn = _rms(k_ref[rows, :], HEAD_DIM) * kn_ref[...]
            v = v_ref[rows, :]
            if latent:
                kn = _rope(kn, cosk_ref[...], slok_ref[...], shik_ref[...])
                ctx_rows = pl.ds(kv, n_ctx, stride=N_KV_HEADS)
                k_scr[s, 0:n_ctx, :] = ck_ref[ctx_rows, :].astype(BF16)
                v_scr[s, 0:n_ctx, :] = cv_ref[ctx_rows, :].astype(BF16)
            else:
                new_rows = pl.ds(s * t_len * N_KV_HEADS + kv, t_len, stride=N_KV_HEADS)
                ko_ref[new_rows, :] = kn
                vo_ref[new_rows, :] = v
            k_scr[s, n_ctx:n_ctx + t_len, :] = kn.astype(BF16)
            v_scr[s, n_ctx:n_ctx + t_len, :] = v.astype(BF16)

    qn = qn_ref[...]
    q_scale = (HEAD_DIM ** -0.5) * LOG2_E
    sub = min(ATTN_SUB_ROWS, tq)
    chains = [(s, s * tq + r0, slice(h * HEAD_DIM, (h + 1) * HEAD_DIM))
              for s in range(n_seq) for r0 in range(0, tq, sub) for h in range(Q_PER_KV)]

    def scores(s, r0, cols):
        rows = slice(r0, r0 + sub)
        qh = _rms(q_ref[rows, cols], HEAD_DIM) * qn
        if latent:
            qh = _rope(qh, cosq_ref[rows, :], sloq_ref[rows, :], shiq_ref[rows, :])
        qh = (qh * q_scale).astype(BF16)
        return lax.dot_general(qh, k_scr[s], (((1,), (1,)), ((), ())), preferred_element_type=F32)

    sc_next = scores(*chains[0])
    for i, (s, r0, cols) in enumerate(chains):
        rows = slice(r0, r0 + sub)
        sc = sc_next
        if i + 1 < len(chains):
            sc_next = scores(*chains[i + 1])
        e = jnp.exp2(sc - jnp.max(sc, axis=-1, keepdims=True))
        denom = jnp.sum(e, axis=-1, keepdims=True)
        o = jnp.dot(e.astype(BF16), v_scr[s], preferred_element_type=F32) / denom
        og_ref[rows, cols] = (o * _silu(g_ref[rows, cols])).astype(BF16)


def _attention(p, n_batch, t_len, q_norm, k_norm, cache=None, rope=None, casts=(), tq=1024, n_seq=1):
    tq = min(tq, t_len)
    assert n_seq == 1 or tq == t_len
    rows = n_batch * t_len
    nq = t_len // tq
    gw = Q_PER_KV * HEAD_DIM
    k_col = ATTN_WIDTH // HEAD_DIM
    v_col = (ATTN_WIDTH + KV_WIDTH) // HEAD_DIM
    g_col = (ATTN_WIDTH + 2 * KV_WIDTH) // gw
    latent = cache is not None
    n_ctx = cache[0].shape[0] // (n_batch * N_KV_HEADS) if latent else 0

    in_specs = [
        pl.BlockSpec((n_seq * tq, gw), lambda b, kv, qi: (b * nq + qi, kv)),
        pl.BlockSpec((n_seq * t_len, HEAD_DIM), lambda b, kv, qi: (b, k_col + kv)),
        pl.BlockSpec((n_seq * t_len, HEAD_DIM), lambda b, kv, qi: (b, v_col + kv)),
        pl.BlockSpec((n_seq * tq, gw), lambda b, kv, qi: (b * nq + qi, g_col + kv)),
    ]
    args = [p, p, p, p]
    if latent:
        in_specs += [pl.BlockSpec((n_ctx * N_KV_HEADS, HEAD_DIM), lambda b, kv, qi: (b, 0))] * 2
        args += list(cache)
    in_specs += [pl.BlockSpec((1, HEAD_DIM), lambda b, kv, qi: (0, 0))] * 2
    args += [q_norm, k_norm]
    og_spec = pl.BlockSpec((n_seq * tq, gw), lambda b, kv, qi: (b * nq + qi, kv))
    og_shape = jax.ShapeDtypeStruct((rows, ATTN_WIDTH), BF16)
    if latent:
        in_specs += [pl.BlockSpec((tq, HEAD_DIM), lambda b, kv, qi: (qi, 0))] * 3
        in_specs += [pl.BlockSpec((t_len, HEAD_DIM), lambda b, kv, qi: (0, 0))] * 3
        args += list(rope) + list(rope)
        out_specs = [og_spec]
        out_shape = [og_shape]
    else:
        kv_spec = pl.BlockSpec((n_seq * t_len * N_KV_HEADS, HEAD_DIM), lambda b, kv, qi: (b, 0))
        kv_shape = jax.ShapeDtypeStruct((rows * N_KV_HEADS, HEAD_DIM), F32)
        out_specs = [og_spec, kv_spec, kv_spec]
        out_shape = [og_shape, kv_shape, kv_shape]
    grid = (n_batch // n_seq, N_KV_HEADS, nq)
    cast_specs = _cast_specs(casts, grid, lambda b, kv, qi: (b * N_KV_HEADS + kv) * nq + qi)
    in_specs = in_specs + cast_specs
    args = args + list(casts)
    out_specs = out_specs + cast_specs
    out_shape = out_shape + [jax.ShapeDtypeStruct(m.shape, BF16) for m in casts]
    return pl.pallas_call(
        functools.partial(_attn_kernel, t_len=t_len, tq=tq, n_ctx=n_ctx, n_seq=n_seq, n_cast=len(casts),
                          latent=latent),
        grid=grid,
        in_specs=in_specs,
        out_specs=out_specs,
        out_shape=out_shape,
        scratch_shapes=[pltpu.VMEM((n_seq, n_ctx + t_len, HEAD_DIM), BF16)] * 2,
        compiler_params=_params(("arbitrary", "arbitrary", "arbitrary")),
        name="attn_latent" if latent else "attn_context",
    )(*args)


def _rope_tables(n):
    rows = n // GRID_W
    row = jnp.repeat(jnp.arange(rows, dtype=F32), GRID_W)
    col = jnp.tile(jnp.arange(GRID_W, dtype=F32), rows)
    inv = ROPE_THETA ** (-jnp.arange(0, AXIS_DIM, 2, dtype=F32) / AXIS_DIM)
    ar = row[:, None] * inv
    ac = col[:, None] * inv
    ang = jnp.concatenate([ar, ar, ac, ac], axis=-1)
    cos, sin = jnp.cos(ang), jnp.sin(ang)
    first = (jnp.arange(HEAD_DIM) % AXIS_DIM) < AXIS_DIM // 2
    return cos, jnp.where(first, -sin, 0.0), jnp.where(first, 0.0, sin)


OUT_SUB_ROWS = 256


def _outproj_kernel(x_ref, w_ref, y_ref, mod_ref, g_ref, nmod_ref, ng_ref, o_ref, h_ref, *, tm, chunk):
    g = g_ref[...]
    gate = mod_ref[0, :, 2 * D_MODEL:3 * D_MODEL]
    ng = ng_ref[...]
    nshift = nmod_ref[0, :, 0:D_MODEL]
    nscale1 = 1.0 + nmod_ref[0, :, D_MODEL:2 * D_MODEL]

    def project(r0):
        return jnp.dot(x_ref[r0:r0 + OUT_SUB_ROWS, :], w_ref[...], preferred_element_type=F32)

    m_next = project(0)
    for r0 in range(0, tm, OUT_SUB_ROWS):
        m = m_next
        if r0 + OUT_SUB_ROWS < tm:
            m_next = project(r0 + OUT_SUB_ROWS)
        for c0 in range(0, OUT_SUB_ROWS, chunk):
            rows = slice(r0 + c0, r0 + c0 + chunk)
            y_new = y_ref[rows, :] + gate * (_rms(m[c0:c0 + chunk, :], D_MODEL) * g)
            o_ref[rows, :] = y_new
            h_ref[rows, :] = _norm_mod(y_new, ng, nscale1, nshift).astype(BF16)


def _outproj(x, w, y, mod, g, next_mod, next_g, rows_per_mod, tm=512, chunk=32):
    rows, d = y.shape
    k = x.shape[1]
    tiles_per_mod = rows_per_mod // tm
    mod_spec = pl.BlockSpec((1, 1, 3 * d), lambda i: (i // tiles_per_mod, 0, 0))
    row_spec = pl.BlockSpec((tm, d), lambda i: (i, 0))
    vec_spec = pl.BlockSpec((1, d), lambda i: (0, 0))
    return pl.pallas_call(
        functools.partial(_outproj_kernel, tm=tm, chunk=chunk),
        grid=(rows // tm,),
        in_specs=[
            pl.BlockSpec((tm, k), lambda i: (i, 0)),
            pl.BlockSpec((k, d), lambda i: (0, 0)),
            row_spec, mod_spec, vec_spec, mod_spec, vec_spec,
        ],
        out_specs=(row_spec, row_spec),
        out_shape=(jax.ShapeDtypeStruct((rows, d), F32), jax.ShapeDtypeStruct((rows, d), BF16)),
        compiler_params=_params(("arbitrary",)),
        name="attn_outproj",
    )(x, w, y, mod, g, next_mod, next_g)


PERM_ROWS = 32
SCAN_STEPS = 8
PROJ_ROWS = 256
PROJ_K_SPLITS = 1
HALF_SLABS = LRU_BLOCK_DIM // LANES


def _log_sigmoid(x):
    return -(jnp.maximum(-x, 0.0) + jnp.log1p(jnp.exp(-jnp.abs(x))))


def _lru_kernel(*refs, n_batch, t_len, chunk, with_final):
    (h_ref, wxb_ref, wgb_ref, cw_ref, cb_ref, wa_ref, ba_ref, wx_ref, bx_ref, lam_ref, h0_ref) = refs[:11]
    if with_final:
        y_ref, fin_ref, x_scr, g_scr, a_scr, b_scr, xc_scr, w16_scr = refs[11:]
    else:
        y_ref, x_scr, g_scr, a_scr, b_scr, xc_scr, w16_scr = refs[11:]
    rows = n_batch * t_len
    c = LRU_BLOCK_DIM
    pad_lo = CONV_LEFT * n_batch
    pad_hi = CONV_RIGHT * n_batch
    step = pl.program_id(0)
    fill = step % 2
    drain = 1 - fill
    seg = min(PROJ_ROWS, t_len)
    k_piece = h_ref.shape[1] // PROJ_K_SPLITS

    def project_piece(i, part, split):
        w_ref = wgb_ref if part else wxb_ref
        cols = slice(split * k_piece, (split + 1) * k_piece)
        lhs = h_ref[pl.ds(pl.multiple_of(i * PROJ_ROWS, PROJ_ROWS), PROJ_ROWS), cols]
        return jnp.dot(lhs, w_ref[cols, :], preferred_element_type=F32)

    def project_store(i, part, res):
        for q in range(PROJ_ROWS // seg):
            first = i * PROJ_ROWS + q * seg
            start = (first % t_len) * n_batch + first // t_len
            for s in range(HALF_SLABS):
                val = res[q * seg:(q + 1) * seg, s * LANES:(s + 1) * LANES]
                if part:
                    g_scr[HALF_SLABS * fill + s, pl.ds(start, seg, stride=n_batch), :] = val
                else:
                    x_scr[s, pl.ds(pad_lo + start, seg, stride=n_batch), :] = val

    @pl.when(step == 0)
    def _():
        for s in range(HALF_SLABS):
            x_scr[s, 0:pad_lo, :] = jnp.zeros((pad_lo, LANES), F32)
            x_scr[s, pad_lo + rows:pad_lo + rows + pad_hi, :] = jnp.zeros((pad_hi, LANES), F32)

        def only_project(i, carry):
            for part in range(2):
                res = project_piece(i, part, 0)
                for split in range(1, PROJ_K_SPLITS):
                    res = res + project_piece(i, part, split)
                project_store(i, part, res)
            return carry
        lax.fori_loop(0, rows // PROJ_ROWS, only_project, 0)

    @pl.when(step > 0)
    def _():
        _lru_block(project_piece, project_store, drain, cw_ref, cb_ref, wa_ref, ba_ref, wx_ref, bx_ref, lam_ref, h0_ref,
                   y_ref, fin_ref if with_final else None, x_scr, g_scr, a_scr, b_scr, xc_scr, w16_scr,
                   n_batch=n_batch, t_len=t_len, chunk=chunk)


def _lru_block(project_piece, project_store, drain, cw_ref, cb_ref, wa_ref, ba_ref, wx_ref, bx_ref, lam_ref, h0_ref,
               y_ref, fin_ref, x_scr, g_scr, a_scr, b_scr, xc_scr, w16_scr, *, n_batch, t_len, chunk):
    rows = n_batch * t_len
    c = LRU_BLOCK_DIM

    for r0 in range(0, rows, chunk):
        for s in range(HALF_SLABS):
            lanes = slice(s * LANES, (s + 1) * LANES)
            acc = cb_ref[:, lanes]
            for j in range(CONV_WIDTH):
                acc = acc + x_scr[s, r0 + j * n_batch:r0 + j * n_batch + chunk, :] * cw_ref[j:j + 1, lanes]
            xc_scr[r0:r0 + chunk, lanes] = acc

    for d in range(2):
        w16_scr[2 * d] = (0.5 * wa_ref[d, 0]).astype(BF16)
        w16_scr[2 * d + 1] = (0.5 * wx_ref[d, 0]).astype(BF16)
    half_ba = [0.5 * ba_ref[d:d + 1, :] for d in range(2)]
    half_bx = [0.5 * bx_ref[d:d + 1, :] for d in range(2)]
    neg_log_s = [-(RG_C * _log_sigmoid(lam_ref[d:d + 1, :])) for d in range(2)]
    exp2_s = [-LOG2_E * neg_log_s[d] for d in range(2)]

    pieces = [(part, split) for part in range(2) for split in range(PROJ_K_SPLITS)]
    assert PROJ_ROWS // chunk * 2 == len(pieces)

    def coeffs(k, carry):
        partial = None
        for sub in range(PROJ_ROWS // chunk):
            r0 = pl.multiple_of(k * PROJ_ROWS + sub * chunk, chunk)
            x = xc_scr[pl.ds(r0, chunk), :]
            x16 = x.astype(BF16)
            for d in range(2):
                zr = jnp.dot(x16, w16_scr[2 * d], preferred_element_type=F32)
                zi = jnp.dot(x16, w16_scr[2 * d + 1], preferred_element_type=F32)
                part, split = pieces[2 * sub + d]
                piece = project_piece(k, part, split)
                partial = piece if split == 0 else partial + piece
                if split == PROJ_K_SPLITS - 1:
                    project_store(k, part, partial)
                r = 0.5 * jnp.tanh(zr + half_ba[d]) + 0.5
                i = 0.5 * jnp.tanh(zi + half_bx[d]) + 0.5
                a = jnp.exp2(r * exp2_s[d])
                m2 = jnp.tanh(r * neg_log_s[d]) * (a * a + 1.0)
                mult = jnp.where(m2 > 0.0, m2 * lax.rsqrt(m2), 0.0)
                a_scr[d, pl.ds(r0, chunk), :] = a
                b_scr[d, pl.ds(r0, chunk), :] = (mult * i) * x
        return carry
    lax.fori_loop(0, rows // PROJ_ROWS, coeffs, 0, unroll=4)

    hf_scr = xc_scr
    hb_scr = b_scr.at[1]
    if n_batch == 16:
        blk = SCAN_STEPS * 16

        def steps(kb, carry):
            hf, hb = carry
            base_f = pl.multiple_of(kb * blk, blk)
            base_b = pl.multiple_of(rows - blk - kb * blk, blk)
            af = a_scr[0, pl.ds(base_f, blk), :]
            bf = b_scr[0, pl.ds(base_f, blk), :]
            ab = a_scr[1, pl.ds(base_b, blk), :]
            bb = b_scr[1, pl.ds(base_b, blk), :]
            hfs, hbs = [], []
            for j in range(SCAN_STEPS):
                rf = slice(16 * j, 16 * (j + 1))
                rb = slice(blk - 16 * (j + 1), blk - 16 * j)
                hf = af[rf] * hf + bf[rf]
                hb = ab[rb] * hb + bb[rb]
                hfs.append(hf)
                hbs.append(hb)
            hf_scr[pl.ds(base_f, blk), :] = jnp.concatenate(hfs, axis=0)
            hb_scr[pl.ds(base_b, blk), :] = jnp.concatenate(hbs[::-1], axis=0)
            return hf, hb
        hf, hb = lax.fori_loop(0, t_len // SCAN_STEPS, steps, (h0_ref[0], h0_ref[1]))
        if fin_ref is not None:
            fin_ref[0] = hf
            fin_ref[1] = hb
    else:
        lower = lax.broadcasted_iota(jnp.int32, (8, c), 0) < 4

        blk = SCAN_STEPS * 8

        def steps(kb, carry):
            cf, cb = carry
            base_f = pl.multiple_of(kb * blk, blk)
            base_b = pl.multiple_of(rows - blk - kb * blk, blk)
            a_f = a_scr[0, pl.ds(base_f, blk), :]
            b_f = b_scr[0, pl.ds(base_f, blk), :]
            a_b = a_scr[1, pl.ds(base_b, blk), :]
            b_b = b_scr[1, pl.ds(base_b, blk), :]
            hfs, hbs = [], []
            for j in range(SCAN_STEPS):
                rf = slice(8 * j, 8 * (j + 1))
                rb = slice(blk - 8 * (j + 1), blk - 8 * j)
                af, bf, ab, bb = a_f[rf], b_f[rf], a_b[rb], b_b[rb]
                uf = af * cf + bf
                ub = ab * cb + bb
                cf = pltpu.roll(af, 4, 0) * uf + pltpu.roll(bf, 4, 0)
                cb = pltpu.roll(ab, 4, 0) * ub + pltpu.roll(bb, 4, 0)
                hfs.append(jnp.where(lower, uf, pltpu.roll(cf, 4, 0)))
                hbs.append(jnp.where(lower, pltpu.roll(cb, 4, 0), ub))
            hf_scr[pl.ds(base_f, blk), :] = jnp.concatenate(hfs, axis=0)
            hb_scr[pl.ds(base_b, blk), :] = jnp.concatenate(hbs[::-1], axis=0)
            return cf, cb
        lax.fori_loop(0, rows // blk, steps, (h0_ref[0], h0_ref[1]))

    def emit(r0):
        h = hf_scr[pl.ds(r0, chunk), :] + hb_scr[pl.ds(r0, chunk), :]
        gate = jnp.concatenate(
            [g_scr[HALF_SLABS * drain + s, pl.ds(r0, chunk), :] for s in range(HALF_SLABS)], axis=-1)
        y_ref[pl.ds(r0, chunk), :] = (h * _silu(gate)).astype(BF16)
    _row_loop(rows, chunk, emit)


def _lru(h, w_in, n_batch, t_len, conv_w, conv_b, w_a, b_a, w_x, b_x, lam, h0, with_final, chunk=256):
    rows, d = h.shape
    c = LRU_BLOCK_DIM
    h_rows = h0.shape[1]
    pad = (CONV_WIDTH - 1) * n_batch
    last = LRU_BLOCKS - 1
    nxt = lambda i: jnp.minimum(i, last)
    cur = lambda i: jnp.maximum(i - 1, 0)
    in_specs = [
        pl.BlockSpec((rows, d), lambda i: (0, 0), pipeline_mode=pl.Buffered(1)),
        pl.BlockSpec((d, c), lambda i: (0, nxt(i))),
        pl.BlockSpec((d, c), lambda i: (0, LRU_BLOCKS + nxt(i))),
        pl.BlockSpec((CONV_WIDTH, c), lambda i: (0, cur(i))),
        pl.BlockSpec((1, c), lambda i: (0, cur(i))),
        pl.BlockSpec((2, 1, c, c), lambda i: (0, cur(i), 0, 0)),
        pl.BlockSpec((2, c), lambda i: (0, cur(i))),
        pl.BlockSpec((2, 1, c, c), lambda i: (0, cur(i), 0, 0)),
        pl.BlockSpec((2, c), lambda i: (0, cur(i))),
        pl.BlockSpec((2, c), lambda i: (0, cur(i))),
        pl.BlockSpec((2, h_rows, c), lambda i: (0, 0, cur(i))),
    ]
    y_spec = pl.BlockSpec((rows, c), lambda i: (0, cur(i)))
    y_shape = jax.ShapeDtypeStruct((rows, LRU_WIDTH), BF16)
    if with_final:
        out_specs = (y_spec, pl.BlockSpec((2, n_batch, c), lambda i: (0, 0, cur(i))))
        out_shape = (y_shape, jax.ShapeDtypeStruct((2, n_batch, LRU_WIDTH), F32))
    else:
        out_specs = y_spec
        out_shape = y_shape
    return pl.pallas_call(
        functools.partial(_lru_kernel, n_batch=n_batch, t_len=t_len, chunk=chunk, with_final=with_final),
        grid=(LRU_BLOCKS + 1,),
        in_specs=in_specs,
        out_specs=out_specs,
        out_shape=out_shape,
        scratch_shapes=[
            pltpu.VMEM((HALF_SLABS, rows + pad, LANES), F32),
            pltpu.VMEM((2 * HALF_SLABS, rows, LANES), F32),
            pltpu.VMEM((2, rows, c), F32),
            pltpu.VMEM((2, rows, c), F32),
            pltpu.VMEM((rows, c), F32),
            pltpu.VMEM((4, c, c), BF16),
        ],
        compiler_params=_params(("arbitrary",)),
        name="lru_mixer",
    )(h, w_in, w_in, conv_w, conv_b, w_a, b_a, w_x, b_x, lam, h0)


MXU_COLS = 256


def _outproj_perm_kernel(x_ref, w_ref, y_ref, mod_ref, g_ref, o_ref, *slab_scrs, n_batch, tt, n_mod):
    g = g_ref[...]
    sub_t = OUT_SUB_ROWS // n_batch
    perm = min(PERM_ROWS, sub_t)
    def project(blk):
        rows = slice(blk * OUT_SUB_ROWS, (blk + 1) * OUT_SUB_ROWS)
        for n0 in range(0, D_MODEL, MXU_COLS):
            m = jnp.dot(x_ref[rows, :], w_ref[:, n0:n0 + MXU_COLS], preferred_element_type=F32)
            for c in range(MXU_COLS // LANES):
                slab_scrs[blk][n0 // LANES + c] = m[:, c * LANES:(c + 1) * LANES]

    project(0)
    for blk, slab_scr in enumerate(slab_scrs):
        if blk + 1 < len(slab_scrs):
            project(blk + 1)
        t0 = blk * sub_t
        for b in range(n_batch):
            gate = mod_ref[b if n_mod > 1 else 0][:, 2 * D_MODEL:3 * D_MODEL]
            for r0 in range(0, sub_t, perm):
                m = jnp.concatenate(
                    [slab_scr[c, pl.ds(r0 * n_batch + b, perm, stride=n_batch), :] for c in range(N_SLABS)],
                    axis=-1)
                out_rows = slice(t0 + r0, t0 + r0 + perm)
                o_ref[b, out_rows, :] = y_ref[b, out_rows, :] + gate * (_rms(m, D_MODEL) * g)


def _outproj_perm(x, w, y3, mod, g, tm=512):
    n_batch, t_len, d = y3.shape
    k = x.shape[1]
    tt = tm // n_batch
    n_mod = mod.shape[0]
    return pl.pallas_call(
        functools.partial(_outproj_perm_kernel, n_batch=n_batch, tt=tt, n_mod=n_mod),
        grid=(t_len // tt,),
        in_specs=[
            pl.BlockSpec((tm, k), lambda i: (i, 0)),
            pl.BlockSpec((k, d), lambda i: (0, 0)),
            pl.BlockSpec((n_batch, tt, d), lambda i: (0, i, 0)),
            pl.BlockSpec((n_mod, 1, 3 * d), lambda i: (0, 0, 0)),
            pl.BlockSpec((1, d), lambda i: (0, 0)),
        ],
        out_specs=pl.BlockSpec((n_batch, tt, d), lambda i: (0, i, 0)),
        out_shape=jax.ShapeDtypeStruct((n_batch, t_len, d), F32),
        scratch_shapes=[pltpu.VMEM((N_SLABS, OUT_SUB_ROWS, LANES), F32)] * (tm // OUT_SUB_ROWS),
        compiler_params=_params(("arbitrary",)),
        name="lru_outproj",
    )(x, w, y3, mod, g)


def kernel(x_prompt, x_sample, c, cache_k, cache_v, state_lru, c_ctx, w_mod, b_mod, g_pre, g_post,
           w_in_attn, q_norm, k_norm, w_out_attn, w_in_lru, conv_w, conv_b,
           w_rg_a, b_rg_a, w_rg_x, b_rg_x, lru_lambda, w_out_lru):
    n_p, t_p, d = x_prompt.shape
    n_s, t_s, _ = x_sample.shape
    depth = w_mod.shape[0]
    n_ctx = cache_k.shape[2]

    cond = jnp.concatenate([c_ctx[None, :], c, jnp.zeros((COND_ROWS - 1 - n_s, d), F32)], axis=0)
    mod = _modulation(cond, w_mod, b_mod)

    y_p = x_prompt
    y_s = x_sample
    new_k, new_v, new_h = [], [], []
    for l in range(depth):
        j = l // 2
        mod_p = mod[l, 0:1].reshape(1, 1, 3 * d)
        mod_s = mod[l, 1:1 + n_s].reshape(n_s, 1, 3 * d)
        g_pre_l = g_pre[l].reshape(1, d)
        g_post_l = g_post[l].reshape(1, d)
        if l % 2 == 0:
            w_in = w_in_attn[j].astype(BF16)
            qn = q_norm[j].reshape(1, HEAD_DIM)
            kn = k_norm[j].reshape(1, HEAD_DIM)
            yp2 = y_p.reshape(n_p * t_p, d)
            ys2 = y_s.reshape(n_s * t_s, d)
            pp, = _inproj(yp2, mod_p, g_pre_l, w_in, rows_per_mod=n_p * t_p)
            ps, = _inproj(ys2, mod_s, g_pre_l, w_in, rows_per_mod=t_s)
            has_lru = l + 1 < depth
            og_p, k_p, v_p, w_out = _attention(pp, n_p, t_p, qn, kn, casts=(w_out_attn[j],), n_seq=4)
            ck = cache_k[:, j].reshape(n_s * n_ctx * N_KV_HEADS, HEAD_DIM)
            cv = cache_v[:, j].reshape(n_s * n_ctx * N_KV_HEADS, HEAD_DIM)
            og_s, *lru_w = _attention(ps, n_s, t_s, qn, kn, cache=(ck, cv), rope=_rope_tables(t_s),
                                      casts=(w_in_lru[j], w_out_lru[j]) if has_lru else ())
            nl = min(l + 1, depth - 1)
            nmod_p = mod[nl, 0:1].reshape(1, 1, 3 * d)
            nmod_s = mod[nl, 1:1 + n_s].reshape(n_s, 1, 3 * d)
            ng = g_pre[nl].reshape(1, d)
            y_p, h_p = _outproj(og_p, w_out, yp2, mod_p, g_post_l, nmod_p, ng, rows_per_mod=n_p * t_p)
            y_s, h_s = _outproj(og_s, w_out, ys2, mod_s, g_post_l, nmod_s, ng, rows_per_mod=t_s)
            y_p = y_p.reshape(n_p, t_p, d)
            y_s = y_s.reshape(n_s, t_s, d)
            new_k.append(k_p.reshape(n_p, t_p, N_KV_HEADS, HEAD_DIM))
            new_v.append(v_p.reshape(n_p, t_p, N_KV_HEADS, HEAD_DIM))
        else:
            w_in, w_out = lru_w
            cb = conv_b[j].reshape(1, LRU_WIDTH)
            lru_args = (conv_w[j], cb, w_rg_a[j], b_rg_a[j], w_rg_x[j], b_rg_x[j], lru_lambda[j])
            h0_p = jnp.zeros((2, n_p, LRU_WIDTH), F32)
            h0_s = jnp.swapaxes(state_lru[:, j], 0, 1)
            h0_s = jnp.concatenate([h0_s, h0_s], axis=1)
            yg_p, fin = _lru(h_p, w_in, n_p, t_p, *lru_args, h0_p, with_final=True)
            yg_s = _lru(h_s, w_in, n_s, t_s, *lru_args, h0_s, with_final=False)
            y_p = _outproj_perm(yg_p, w_out, y_p, mod_p, g_post_l)
            y_s = _outproj_perm(yg_s, w_out, y_s, mod_s, g_post_l)
            new_h.append(jnp.swapaxes(fin, 0, 1))
    new_cache_k = jnp.stack(new_k, axis=1)
    new_cache_v = jnp.stack(new_v, axis=1)
    new_state_lru = jnp.stack(new_h, axis=1)
    return (y_p, y_s, new_cache_k, new_cache_v, new_state_lru)
```

```python
import functools

import jax
import jax.numpy as jnp
from jax import lax
from jax.experimental import pallas as pl
from jax.experimental.pallas import tpu as pltpu

F32 = jnp.float32
BF16 = jnp.bfloat16

D_MODEL = 2048
HEAD_DIM = 128
N_HEADS = 16
N_KV_HEADS = 4
Q_PER_KV = N_HEADS // N_KV_HEADS
ATTN_WIDTH = N_HEADS * HEAD_DIM
KV_WIDTH = N_KV_HEADS * HEAD_DIM
GRID_W = 64
AXIS_DIM = HEAD_DIM // 2
ROPE_THETA = 10000.0
LRU_WIDTH = D_MODEL
LRU_BLOCKS = 8
LRU_BLOCK_DIM = LRU_WIDTH // LRU_BLOCKS
CONV_WIDTH = 4
CONV_LEFT = (CONV_WIDTH - 1) // 2
CONV_RIGHT = CONV_WIDTH - 1 - CONV_LEFT
RG_C = 8.0
EPS = 1e-6
LOG2_E = 1.4426950408889634

LANES = 128
N_SLABS = D_MODEL // LANES
COND_ROWS = 8
VMEM_LIMIT = 60 * 1024 * 1024


def _params(sem):
    return pltpu.CompilerParams(dimension_semantics=sem, vmem_limit_bytes=VMEM_LIMIT)


def _row_loop(n_rows, chunk, body, unroll=1):
    def step(i, carry):
        body(pl.multiple_of(i * chunk, chunk))
        return carry
    lax.fori_loop(0, n_rows // chunk, step, 0, unroll=unroll)


def _rms(x, width):
    return x * lax.rsqrt(jnp.sum(x * x, axis=-1, keepdims=True) * (1.0 / width) + EPS)


def _sigmoid(x):
    return 0.5 * jnp.tanh(0.5 * x) + 0.5


def _silu(x):
    return x * _sigmoid(x)


def _mod_kernel(cond_ref, w_ref, b_ref, o_ref):
    x = _silu(cond_ref[...]).astype(BF16)
    o_ref[0] = jnp.dot(x, w_ref[0].astype(BF16), preferred_element_type=F32) + b_ref[0]


def _modulation(cond, w_mod, b_mod, layer_stride, tn=1024):
    depth, d, n = w_mod.shape
    n_layers = -(-depth // layer_stride)
    return pl.pallas_call(
        _mod_kernel,
        grid=(n_layers, n // tn),
        in_specs=[
            pl.BlockSpec((COND_ROWS, d), lambda l, j: (0, 0)),
            pl.BlockSpec((1, d, tn), lambda l, j: (l * layer_stride, 0, j)),
            pl.BlockSpec((1, 1, tn), lambda l, j: (l * layer_stride, 0, j)),
        ],
        out_specs=pl.BlockSpec((1, COND_ROWS, tn), lambda l, j: (l, 0, j)),
        out_shape=jax.ShapeDtypeStruct((n_layers, COND_ROWS, n), F32),
        compiler_params=_params(("arbitrary", "arbitrary")),
        name="modulation",
    )(cond, w_mod, b_mod)


def _norm_mod(x, g, scale1, shift):
    return (_rms(x, D_MODEL) * g) * scale1 + shift


CAST_ROWS = 128


def _cast_specs(casts, grid, step_of):
    n_steps = 1
    for extent in grid:
        n_steps *= extent
    specs = []
    for m in casts:
        tiles = m.shape[0] // CAST_ROWS
        assert tiles * CAST_ROWS == m.shape[0] and tiles <= n_steps
        specs.append(pl.BlockSpec(
            (CAST_ROWS, m.shape[1]), lambda *idx, tiles=tiles: (jnp.minimum(step_of(*idx), tiles - 1), 0)))
    return specs


def _inproj_kernel(*refs, n_cast, tm, chunk):
    y_ref, mod_ref, g_ref, w_ref = refs[:4]
    cast_src = refs[4:4 + n_cast]
    o_ref = refs[4 + n_cast]
    cast_dst = refs[5 + n_cast:5 + 2 * n_cast]
    h_scr = refs[5 + 2 * n_cast]

    @pl.when(pl.program_id(1) == 0)
    def _():
        g = g_ref[...]
        shift = mod_ref[0, :, 0:D_MODEL]
        scale1 = 1.0 + mod_ref[0, :, D_MODEL:2 * D_MODEL]

        def body(r0):
            x = y_ref[pl.ds(r0, chunk), :]
            h_scr[pl.ds(r0, chunk), :] = _norm_mod(x, g, scale1, shift).astype(BF16)
        _row_loop(tm, chunk, body, unroll=4)

    o_ref[...] = jnp.dot(h_scr[...], w_ref[...], preferred_element_type=F32)
    for src, dst in zip(cast_src, cast_dst):
        dst[...] = src[...].astype(BF16)


def _inproj(y, mod, g, w, rows_per_mod, casts=(), tm=1024, tn=1024, chunk=32):
    rows, d = y.shape
    n = w.shape[1]
    tiles_per_mod = rows_per_mod // tm
    n_j = n // tn
    grid = (rows // tm, n_j)
    cast_specs = _cast_specs(casts, grid, lambda i, j: i * n_j + j)
    return pl.pallas_call(
        functools.partial(_inproj_kernel, n_cast=len(casts), tm=tm, chunk=chunk),
        grid=grid,
        in_specs=[
            pl.BlockSpec((tm, d), lambda i, j: (i, 0)),
            pl.BlockSpec((1, 1, 3 * d), lambda i, j: (i // tiles_per_mod, 0, 0)),
            pl.BlockSpec((1, d), lambda i, j: (0, 0)),
            pl.BlockSpec((d, tn), lambda i, j: (0, j)),
        ] + cast_specs,
        out_specs=[pl.BlockSpec((tm, tn), lambda i, j: (i, j))] + cast_specs,
        out_shape=[jax.ShapeDtypeStruct((rows, n), F32)] + [jax.ShapeDtypeStruct(m.shape, BF16) for m in casts],
        scratch_shapes=[pltpu.VMEM((tm, d), BF16)],
        compiler_params=_params(("arbitrary", "arbitrary")),
        name="attn_inproj",
    )(y, mod, g, w, *casts)


ATTN_SUB_ROWS = 256


def _rope(x, cos, sin_lo, sin_hi):
    return x * cos + pltpu.roll(x, HEAD_DIM - AXIS_DIM // 2, 1) * sin_lo + pltpu.roll(x, AXIS_DIM // 2, 1) * sin_hi


def _attn_kernel(*refs, t_len, tq, n_ctx, n_seq, n_cast, side_mod, latent):
    n_in = 14 if latent else 6
    n_side = 3 if side_mod else 0
    n_out = 1 if latent else 3
    cast_src = refs[n_in:n_in + n_cast]
    side_in = refs[n_in + n_cast:n_in + n_cast + n_side]
    outs = refs[n_in + n_cast + n_side:]
    cast_dst = outs[n_out:n_out + n_cast]
    k_scr, v_scr = refs[-2:]
    if latent:
        (q_ref, k_ref, v_ref, g_ref, ck_ref, cv_ref, qn_ref, kn_ref,
         cosq_ref, sloq_ref, shiq_ref, cosk_ref, slok_ref, shik_ref) = refs[:n_in]
        og_ref = outs[0]
    else:
        q_ref, k_ref, v_ref, g_ref, qn_ref, kn_ref = refs[:n_in]
        og_ref, ko_ref, vo_ref = outs[:n_out]
    for src, dst in zip(cast_src, cast_dst):
        dst[...] = src[...].astype(BF16)
    if side_mod:
        _mod_kernel(*side_in, outs[n_out + n_cast])

    kv = pl.program_id(1)

    @pl.when(pl.program_id(2) == 0)
    def _():
        for s in range(n_seq):
            rows = slice(s * t_len, (s + 1) * t_len)
            kn = _rms(k_ref[rows, :], HEAD_DIM) * kn_ref[...]
            v = v_ref[rows, :]
            if latent:
                kn = _rope(kn, cosk_ref[...], slok_ref[...], shik_ref[...])
                ctx_rows = pl.ds(kv, n_ctx, stride=N_KV_HEADS)
                k_scr[s, 0:n_ctx, :] = ck_ref[ctx_rows, :].astype(BF16)
                v_scr[s, 0:n_ctx, :] = cv_ref[ctx_rows, :].astype(BF16)
            else:
                new_rows = pl.ds(s * t_len * N_KV_HEADS + kv, t_len, stride=N_KV_HEADS)
                ko_ref[new_rows, :] = kn
                vo_ref[new_rows, :] = v
            k_scr[s, n_ctx:n_ctx + t_len, :] = kn.astype(BF16)
            v_scr[s, n_ctx:n_ctx + t_len, :] = v.astype(BF16)

    qn = qn_ref[...]
    q_scale = (HEAD_DIM ** -0.5) * LOG2_E
    sub = min(ATTN_SUB_ROWS, tq)
    chains = [(s, s * tq + r0, slice(h * HEAD_DIM, (h + 1) * HEAD_DIM))
              for s in range(n_seq) for r0 in range(0, tq, sub) for h in range(Q_PER_KV)]

    def scores(s, r0, cols):
        rows = slice(r0, r0 + sub)
        qh = _rms(q_ref[rows, cols], HEAD_DIM) * qn
        if latent:
            qh = _rope(qh, cosq_ref[rows, :], sloq_ref[rows, :], shiq_ref[rows, :])
        qh = (qh * q_scale).astype(BF16)
        return lax.dot_general(qh, k_scr[s], (((1,), (1,)), ((), ())), preferred_element_type=F32)

    sc_next = scores(*chains[0])
    for i, (s, r0, cols) in enumerate(chains):
        rows = slice(r0, r0 + sub)
        sc = sc_next
        if i + 1 < len(chains):
            sc_next = scores(*chains[i + 1])
        e = jnp.exp2(sc - jnp.max(sc, axis=-1, keepdims=True))
        denom = jnp.sum(e, axis=-1, keepdims=True)
        o = jnp.dot(e.astype(BF16), v_scr[s], preferred_element_type=F32) / denom
        og_ref[rows, cols] = (o * _silu(g_ref[rows, cols])).astype(BF16)


def _attention(p, n_batch, t_len, q_norm, k_norm, cache=None, rope=None, casts=(), side_mod=None,
               tq=1024, n_seq=1):
    tq = min(tq, t_len)
    assert n_seq == 1 or tq == t_len
    rows = n_batch * t_len
    nq = t_len // tq
    gw = Q_PER_KV * HEAD_DIM
    k_col = ATTN_WIDTH // HEAD_DIM
    v_col = (ATTN_WIDTH + KV_WIDTH) // HEAD_DIM
    g_col = (ATTN_WIDTH + 2 * KV_WIDTH) // gw
    latent = cache is not None
    n_ctx = cache[0].shape[0] // (n_batch * N_KV_HEADS) if latent else 0

    in_specs = [
        pl.BlockSpec((n_seq * tq, gw), lambda b, kv, qi: (b * nq + qi, kv)),
        pl.BlockSpec((n_seq * t_len, HEAD_DIM), lambda b, kv, qi: (b, k_col + kv)),
        pl.BlockSpec((n_seq * t_len, HEAD_DIM), lambda b, kv, qi: (b, v_col + kv)),
        pl.BlockSpec((n_seq * tq, gw), lambda b, kv, qi: (b * nq + qi, g_col + kv)),
    ]
    args = [p, p, p, p]
    if latent:
        in_specs += [pl.BlockSpec((n_ctx * N_KV_HEADS, HEAD_DIM), lambda b, kv, qi: (b, 0))] * 2
        args += list(cache)
    in_specs += [pl.BlockSpec((1, HEAD_DIM), lambda b, kv, qi: (0, 0))] * 2
    args += [q_norm, k_norm]
    og_spec = pl.BlockSpec((n_seq * tq, gw), lambda b, kv, qi: (b * nq + qi, kv))
    og_shape = jax.ShapeDtypeStruct((rows, ATTN_WIDTH), BF16)
    if latent:
        in_specs += [pl.BlockSpec((tq, HEAD_DIM), lambda b, kv, qi: (qi, 0))] * 3
        in_specs += [pl.BlockSpec((t_len, HEAD_DIM), lambda b, kv, qi: (0, 0))] * 3
        args += list(rope) + list(rope)
        out_specs = [og_spec]
        out_shape = [og_shape]
    else:
        kv_spec = pl.BlockSpec((n_seq * t_len * N_KV_HEADS, HEAD_DIM), lambda b, kv, qi: (b, 0))
        kv_shape = jax.ShapeDtypeStruct((rows * N_KV_HEADS, HEAD_DIM), F32)
        out_specs = [og_spec, kv_spec, kv_spec]
        out_shape = [og_shape, kv_shape, kv_shape]
    grid = (n_batch // n_seq, N_KV_HEADS, nq)
    cast_specs = _cast_specs(casts, grid, lambda b, kv, qi: (b * N_KV_HEADS + kv) * nq + qi)
    in_specs = in_specs + cast_specs
    args = args + list(casts)
    out_specs = out_specs + cast_specs
    out_shape = out_shape + [jax.ShapeDtypeStruct(m.shape, BF16) for m in casts]
    if side_mod is not None:
        cond, w_mod, b_mod, layer = side_mod
        n_steps = grid[0] * grid[1] * grid[2]
        d_in, n_mod = w_mod.shape[1:]
        tn_mod = n_mod // n_steps
        assert tn_mod * n_steps == n_mod and tn_mod % LANES == 0
        col = lambda b, kv, qi: (b * N_KV_HEADS + kv) * nq + qi
        in_specs = in_specs + [
            pl.BlockSpec((COND_ROWS, d_in), lambda b, kv, qi: (0, 0)),
            pl.BlockSpec((1, d_in, tn_mod), lambda b, kv, qi: (layer, 0, col(b, kv, qi))),
            pl.BlockSpec((1, 1, tn_mod), lambda b, kv, qi: (layer, 0, col(b, kv, qi))),
        ]
        args = args + [cond, w_mod, b_mod]
        out_specs = out_specs + [pl.BlockSpec((1, COND_ROWS, tn_mod), lambda b, kv, qi: (0, 0, col(b, kv, qi)))]
        out_shape = out_shape + [jax.ShapeDtypeStruct((1, COND_ROWS, n_mod), F32)]
    return pl.pallas_call(
        functools.partial(_attn_kernel, t_len=t_len, tq=tq, n_ctx=n_ctx, n_seq=n_seq, n_cast=len(casts),
                          side_mod=side_mod is not None, latent=latent),
        grid=grid,
        in_specs=in_specs,
        out_specs=out_specs,
        out_shape=out_shape,
        scratch_shapes=[pltpu.VMEM((n_seq, n_ctx + t_len, HEAD_DIM), BF16)] * 2,
        compiler_params=_params(("arbitrary", "arbitrary", "arbitrary")),
        name="attn_latent" if latent else "attn_context",
    )(*args)


def _rope_tables(n):
    rows = n // GRID_W
    row = jnp.repeat(jnp.arange(rows, dtype=F32), GRID_W)
    col = jnp.tile(jnp.arange(GRID_W, dtype=F32), rows)
    inv = ROPE_THETA ** (-jnp.arange(0, AXIS_DIM, 2, dtype=F32) / AXIS_DIM)
    ar = row[:, None] * inv
    ac = col[:, None] * inv
    ang = jnp.concatenate([ar, ar, ac, ac], axis=-1)
    cos, sin = jnp.cos(ang), jnp.sin(ang)
    first = (jnp.arange(HEAD_DIM) % AXIS_DIM) < AXIS_DIM // 2
    return cos, jnp.where(first, -sin, 0.0), jnp.where(first, 0.0, sin)


OUT_SUB_ROWS = 256


def _outproj_kernel(x_ref, w_ref, y_ref, mod_ref, g_ref, nmod_ref, ng_ref, o_ref, h_ref, *, tm, chunk):
    g = g_ref[...]
    gate = mod_ref[0, :, 2 * D_MODEL:3 * D_MODEL]
    ng = ng_ref[...]
    nshift = nmod_ref[0, :, 0:D_MODEL]
    nscale1 = 1.0 + nmod_ref[0, :, D_MODEL:2 * D_MODEL]

    def project(r0):
        return jnp.dot(x_ref[r0:r0 + OUT_SUB_ROWS, :], w_ref[...], preferred_element_type=F32)

    m_next = project(0)
    for r0 in range(0, tm, OUT_SUB_ROWS):
        m = m_next
        if r0 + OUT_SUB_ROWS < tm:
            m_next = project(r0 + OUT_SUB_ROWS)
        for c0 in range(0, OUT_SUB_ROWS, chunk):
            rows = slice(r0 + c0, r0 + c0 + chunk)
            y_new = y_ref[rows, :] + gate * (_rms(m[c0:c0 + chunk, :], D_MODEL) * g)
            o_ref[rows, :] = y_new
            h_ref[rows, :] = _norm_mod(y_new, ng, nscale1, nshift).astype(BF16)


def _outproj(x, w, y, mod, g, next_mod, next_g, rows_per_mod, tm=512, chunk=32):
    rows, d = y.shape
    k = x.shape[1]
    tiles_per_mod = rows_per_mod // tm
    mod_spec = pl.BlockSpec((1, 1, 3 * d), lambda i: (i // tiles_per_mod, 0, 0))
    row_spec = pl.BlockSpec((tm, d), lambda i: (i, 0))
    vec_spec = pl.BlockSpec((1, d), lambda i: (0, 0))
    return pl.pallas_call(
        functools.partial(_outproj_kernel, tm=tm, chunk=chunk),
        grid=(rows // tm,),
        in_specs=[
            pl.BlockSpec((tm, k), lambda i: (i, 0)),
            pl.BlockSpec((k, d), lambda i: (0, 0)),
            row_spec, mod_spec, vec_spec, mod_spec, vec_spec,
        ],
        out_specs=(row_spec, row_spec),
        out_shape=(jax.ShapeDtypeStruct((rows, d), F32), jax.ShapeDtypeStruct((rows, d), BF16)),
        compiler_params=_params(("arbitrary",)),
        name="attn_outproj",
    )(x, w, y, mod, g, next_mod, next_g)


PERM_ROWS = 32
SCAN_STEPS = 8
PROJ_ROWS = 256
PROJ_K_SPLITS = 1
GATE_GROUP = 2
HALF_SLABS = LRU_BLOCK_DIM // LANES


def _log_sigmoid(x):
    return -(jnp.maximum(-x, 0.0) + jnp.log1p(jnp.exp(-jnp.abs(x))))


def _lru_kernel(*refs, n_batch, t_len, chunk, with_final):
    (h_ref, wxb_ref, wgb_ref, cw_ref, cb_ref, wa_ref, ba_ref, wx_ref, bx_ref, lam_ref, h0_ref) = refs[:11]
    if with_final:
        y_ref, fin_ref, x_scr, g_scr, a_scr, b_scr, xc_scr, w16_scr = refs[11:]
    else:
        y_ref, x_scr, g_scr, a_scr, b_scr, xc_scr, w16_scr = refs[11:]
    rows = n_batch * t_len
    c = LRU_BLOCK_DIM
    pad_lo = CONV_LEFT * n_batch
    pad_hi = CONV_RIGHT * n_batch
    step = pl.program_id(0)
    fill = step % 2
    drain = 1 - fill
    seg = min(PROJ_ROWS, t_len)
    k_piece = h_ref.shape[1] // PROJ_K_SPLITS

    def project_piece(i, part, split):
        w_ref = wgb_ref if part else wxb_ref
        cols = slice(split * k_piece, (split + 1) * k_piece)
        lhs = h_ref[pl.ds(pl.multiple_of(i * PROJ_ROWS, PROJ_ROWS), PROJ_ROWS), cols]
        return jnp.dot(lhs, w_ref[cols, :], preferred_element_type=F32)

    def project_store(i, part, res):
        for q in range(PROJ_ROWS // seg):
            first = i * PROJ_ROWS + q * seg
            start = (first % t_len) * n_batch + first // t_len
            for s in range(HALF_SLABS):
                val = res[q * seg:(q + 1) * seg, s * LANES:(s + 1) * LANES]
                if part:
                    g_scr[HALF_SLABS * fill + s, pl.ds(start, seg, stride=n_batch), :] = val
                else:
                    x_scr[s, pl.ds(pad_lo + start, seg, stride=n_batch), :] = val

    @pl.when(step == 0)
    def _():
        for s in range(HALF_SLABS):
            x_scr[s, 0:pad_lo, :] = jnp.zeros((pad_lo, LANES), F32)
            x_scr[s, pad_lo + rows:pad_lo + rows + pad_hi, :] = jnp.zeros((pad_hi, LANES), F32)

        def only_project(i, carry):
            for part in range(2):
                res = project_piece(i, part, 0)
                for split in range(1, PROJ_K_SPLITS):
                    res = res + project_piece(i, part, split)
                project_store(i, part, res)
            return carry
        lax.fori_loop(0, rows // PROJ_ROWS, only_project, 0)

    @pl.when(step > 0)
    def _():
        _lru_block(project_piece, project_store, drain, cw_ref, cb_ref, wa_ref, ba_ref, wx_ref, bx_ref, lam_ref, h0_ref,
                   y_ref, fin_ref if with_final else None, x_scr, g_scr, a_scr, b_scr, xc_scr, w16_scr,
                   n_batch=n_batch, t_len=t_len, chunk=chunk)


def _lru_block(project_piece, project_store, drain, cw_ref, cb_ref, wa_ref, ba_ref, wx_ref, bx_ref, lam_ref, h0_ref,
               y_ref, fin_ref, x_scr, g_scr, a_scr, b_scr, xc_scr, w16_scr, *, n_batch, t_len, chunk):
    rows = n_batch * t_len
    c = LRU_BLOCK_DIM

    for r0 in range(0, rows, chunk):
        for s in range(HALF_SLABS):
            lanes = slice(s * LANES, (s + 1) * LANES)
            acc = cb_ref[:, lanes]
            for j in range(CONV_WIDTH):
                acc = acc + x_scr[s, r0 + j * n_batch:r0 + j * n_batch + chunk, :] * cw_ref[j:j + 1, lanes]
            xc_scr[r0:r0 + chunk, lanes] = acc

    for d in range(2):
        w16_scr[2 * d] = (0.5 * wa_ref[d, 0]).astype(BF16)
        w16_scr[2 * d + 1] = (0.5 * wx_ref[d, 0]).astype(BF16)
    half_ba = [0.5 * ba_ref[d:d + 1, :] for d in range(2)]
    half_bx = [0.5 * bx_ref[d:d + 1, :] for d in range(2)]
    neg_log_s = [-(RG_C * _log_sigmoid(lam_ref[d:d + 1, :])) for d in range(2)]
    exp2_s = [-LOG2_E * neg_log_s[d] for d in range(2)]

    assert chunk == PROJ_ROWS and PROJ_K_SPLITS == 1

    def coeffs(k, carry):
        staged = []
        for sub in range(GATE_GROUP):
            r0 = pl.multiple_of((k * GATE_GROUP + sub) * chunk, chunk)
            x = xc_scr[pl.ds(r0, chunk), :]
            x16 = x.astype(BF16)
            z = [(jnp.dot(x16, w16_scr[2 * d], preferred_element_type=F32),
                  jnp.dot(x16, w16_scr[2 * d + 1], preferred_element_type=F32)) for d in range(2)]
            staged.append((r0, x, z))
        for sub, (r0, x, z) in enumerate(staged):
            for d in range(2):
                i_proj = k * GATE_GROUP + sub
                project_store(i_proj, d, project_piece(i_proj, d, 0))
                zr, zi = z[d]
                r = 0.5 * jnp.tanh(zr + half_ba[d]) + 0.5
                i = 0.5 * jnp.tanh(zi + half_bx[d]) + 0.5
                a = jnp.exp2(r * exp2_s[d])
                m2 = jnp.tanh(r * neg_log_s[d]) * (a * a + 1.0)
                mult = jnp.where(m2 > 0.0, m2 * lax.rsqrt(m2), 0.0)
                a_scr[d, pl.ds(r0, chunk), :] = a
                b_scr[d, pl.ds(r0, chunk), :] = (mult * i) * x
        return carry
    lax.fori_loop(0, rows // (GATE_GROUP * chunk), coeffs, 0, unroll=2)

    hf_scr = xc_scr
    hb_scr = b_scr.at[1]
    if n_batch == 16:
        blk = SCAN_STEPS * 16

        def steps(kb, carry):
            hf, hb = carry
            base_f = pl.multiple_of(kb * blk, blk)
            base_b = pl.multiple_of(rows - blk - kb * blk, blk)
            af = a_scr[0, pl.ds(base_f, blk), :]
            bf = b_scr[0, pl.ds(base_f, blk), :]
            ab = a_scr[1, pl.ds(base_b, blk), :]
            bb = b_scr[1, pl.ds(base_b, blk), :]
            hfs, hbs = [], []
            for j in range(SCAN_STEPS):
                rf = slice(16 * j, 16 * (j + 1))
                rb = slice(blk - 16 * (j + 1), blk - 16 * j)
                hf = af[rf] * hf + bf[rf]
                hb = ab[rb] * hb + bb[rb]
                hfs.append(hf)
                hbs.append(hb)
            hf_scr[pl.ds(base_f, blk), :] = jnp.concatenate(hfs, axis=0)
            hb_scr[pl.ds(base_b, blk), :] = jnp.concatenate(hbs[::-1], axis=0)
            return hf, hb
        hf, hb = lax.fori_loop(0, t_len // SCAN_STEPS, steps, (h0_ref[0], h0_ref[1]))
        if fin_ref is not None:
            fin_ref[0] = hf
            fin_ref[1] = hb
    else:
        lower = lax.broadcasted_iota(jnp.int32, (8, c), 0) < 4

        blk = SCAN_STEPS * 8

        def steps(kb, carry):
            cf, cb = carry
            base_f = pl.multiple_of(kb * blk, blk)
            base_b = pl.multiple_of(rows - blk - kb * blk, blk)
            a_f = a_scr[0, pl.ds(base_f, blk), :]
            b_f = b_scr[0, pl.ds(base_f, blk), :]
            a_b = a_scr[1, pl.ds(base_b, blk), :]
            b_b = b_scr[1, pl.ds(base_b, blk), :]
            hfs, hbs = [], []
            for j in range(SCAN_STEPS):
                rf = slice(8 * j, 8 * (j + 1))
                rb = slice(blk - 8 * (j + 1), blk - 8 * j)
                af, bf, ab, bb = a_f[rf], b_f[rf], a_b[rb], b_b[rb]
                uf = af * cf + bf
                ub = ab * cb + bb
                cf = pltpu.roll(af, 4, 0) * uf + pltpu.roll(bf, 4, 0)
                cb = pltpu.roll(ab, 4, 0) * ub + pltpu.roll(bb, 4, 0)
                hfs.append(jnp.where(lower, uf, pltpu.roll(cf, 4, 0)))
                hbs.append(jnp.where(lower, pltpu.roll(cb, 4, 0), ub))
            hf_scr[pl.ds(base_f, blk), :] = jnp.concatenate(hfs, axis=0)
            hb_scr[pl.ds(base_b, blk), :] = jnp.concatenate(hbs[::-1], axis=0)
            return cf, cb
        lax.fori_loop(0, rows // blk, steps, (h0_ref[0], h0_ref[1]))

    def emit(r0):
        h = hf_scr[pl.ds(r0, chunk), :] + hb_scr[pl.ds(r0, chunk), :]
        gate = jnp.concatenate(
            [g_scr[HALF_SLABS * drain + s, pl.ds(r0, chunk), :] for s in range(HALF_SLABS)], axis=-1)
        y_ref[pl.ds(r0, chunk), :] = (h * _silu(gate)).astype(BF16)
    _row_loop(rows, chunk, emit)


def _lru(h, w_in, n_batch, t_len, conv_w, conv_b, w_a, b_a, w_x, b_x, lam, h0, with_final, chunk=256):
    rows, d = h.shape
    c = LRU_BLOCK_DIM
    h_rows = h0.shape[1]
    pad = (CONV_WIDTH - 1) * n_batch
    last = LRU_BLOCKS - 1
    nxt = lambda i: jnp.minimum(i, last)
    cur = lambda i: jnp.maximum(i - 1, 0)
    in_specs = [
        pl.BlockSpec((rows, d), lambda i: (0, 0), pipeline_mode=pl.Buffered(1)),
        pl.BlockSpec((d, c), lambda i: (0, nxt(i))),
        pl.BlockSpec((d, c), lambda i: (0, LRU_BLOCKS + nxt(i))),
        pl.BlockSpec((CONV_WIDTH, c), lambda i: (0, cur(i))),
        pl.BlockSpec((1, c), lambda i: (0, cur(i))),
        pl.BlockSpec((2, 1, c, c), lambda i: (0, cur(i), 0, 0)),
        pl.BlockSpec((2, c), lambda i: (0, cur(i))),
        pl.BlockSpec((2, 1, c, c), lambda i: (0, cur(i), 0, 0)),
        pl.BlockSpec((2, c), lambda i: (0, cur(i))),
        pl.BlockSpec((2, c), lambda i: (0, cur(i))),
        pl.BlockSpec((2, h_rows, c), lambda i: (0, 0, cur(i))),
    ]
    y_spec = pl.BlockSpec((rows, c), lambda i: (0, cur(i)))
    y_shape = jax.ShapeDtypeStruct((rows, LRU_WIDTH), BF16)
    if with_final:
        out_specs = (y_spec, pl.BlockSpec((2, n_batch, c), lambda i: (0, 0, cur(i))))
        out_shape = (y_shape, jax.ShapeDtypeStruct((2, n_batch, LRU_WIDTH), F32))
    else:
        out_specs = y_spec
        out_shape = y_shape
    return pl.pallas_call(
        functools.partial(_lru_kernel, n_batch=n_batch, t_len=t_len, chunk=chunk, with_final=with_final),
        grid=(LRU_BLOCKS + 1,),
        in_specs=in_specs,
        out_specs=out_specs,
        out_shape=out_shape,
        scratch_shapes=[
            pltpu.VMEM((HALF_SLABS, rows + pad, LANES), F32),
            pltpu.VMEM((2 * HALF_SLABS, rows, LANES), F32),
            pltpu.VMEM((2, rows, c), F32),
            pltpu.VMEM((2, rows, c), F32),
            pltpu.VMEM((rows, c), F32),
            pltpu.VMEM((4, c, c), BF16),
        ],
        compiler_params=_params(("arbitrary",)),
        name="lru_mixer",
    )(h, w_in, w_in, conv_w, conv_b, w_a, b_a, w_x, b_x, lam, h0)


MXU_COLS = 256


def _outproj_perm_kernel(x_ref, w_ref, y_ref, mod_ref, g_ref, o_ref, *slab_scrs, n_batch, tt, n_mod):
    g = g_ref[...]
    sub_t = OUT_SUB_ROWS // n_batch
    perm = min(PERM_ROWS, sub_t)
    def project(blk):
        rows = slice(blk * OUT_SUB_ROWS, (blk + 1) * OUT_SUB_ROWS)
        for n0 in range(0, D_MODEL, MXU_COLS):
            m = jnp.dot(x_ref[rows, :], w_ref[:, n0:n0 + MXU_COLS], preferred_element_type=F32)
            for c in range(MXU_COLS // LANES):
                slab_scrs[blk][n0 // LANES + c] = m[:, c * LANES:(c + 1) * LANES]

    project(0)
    for blk, slab_scr in enumerate(slab_scrs):
        if blk + 1 < len(slab_scrs):
            project(blk + 1)
        t0 = blk * sub_t
        for b in range(n_batch):
            gate = mod_ref[b if n_mod > 1 else 0][:, 2 * D_MODEL:3 * D_MODEL]
            for r0 in range(0, sub_t, perm):
                m = jnp.concatenate(
                    [slab_scr[c, pl.ds(r0 * n_batch + b, perm, stride=n_batch), :] for c in range(N_SLABS)],
                    axis=-1)
                out_rows = slice(t0 + r0, t0 + r0 + perm)
                o_ref[b, out_rows, :] = y_ref[b, out_rows, :] + gate * (_rms(m, D_MODEL) * g)


def _outproj_perm(x, w, y3, mod, g, tm=512):
    n_batch, t_len, d = y3.shape
    k = x.shape[1]
    tt = tm // n_batch
    n_mod = mod.shape[0]
    return pl.pallas_call(
        functools.partial(_outproj_perm_kernel, n_batch=n_batch, tt=tt, n_mod=n_mod),
        grid=(t_len // tt,),
        in_specs=[
            pl.BlockSpec((tm, k), lambda i: (i, 0)),
            pl.BlockSpec((k, d), lambda i: (0, 0)),
            pl.BlockSpec((n_batch, tt, d), lambda i: (0, i, 0)),
            pl.BlockSpec((n_mod, 1, 3 * d), lambda i: (0, 0, 0)),
            pl.BlockSpec((1, d), lambda i: (0, 0)),
        ],
        out_specs=pl.BlockSpec((n_batch, tt, d), lambda i: (0, i, 0)),
        out_shape=jax.ShapeDtypeStruct((n_batch, t_len, d), F32),
        scratch_shapes=[pltpu.VMEM((N_SLABS, OUT_SUB_ROWS, LANES), F32)] * (tm // OUT_SUB_ROWS),
        compiler_params=_params(("arbitrary",)),
        name="lru_outproj",
    )(x, w, y3, mod, g)


def kernel(x_prompt, x_sample, c, cache_k, cache_v, state_lru, c_ctx, w_mod, b_mod, g_pre, g_post,
           w_in_attn, q_norm, k_norm, w_out_attn, w_in_lru, conv_w, conv_b,
           w_rg_a, b_rg_a, w_rg_x, b_rg_x, lru_lambda, w_out_lru):
    n_p, t_p, d = x_prompt.shape
    n_s, t_s, _ = x_sample.shape
    depth = w_mod.shape[0]
    n_ctx = cache_k.shape[2]

    cond = jnp.concatenate([c_ctx[None, :], c, jnp.zeros((COND_ROWS - 1 - n_s, d), F32)], axis=0)
    b_mod3 = b_mod.reshape(depth, 1, 3 * d)
    mod_attn = _modulation(cond, w_mod, b_mod3, layer_stride=2)

    def mod_rows(m):
        return m[0:1].reshape(1, 1, 3 * d), m[1:1 + n_s].reshape(n_s, 1, 3 * d)

    y_p = x_prompt
    y_s = x_sample
    new_k, new_v, new_h = [], [], []
    for l in range(depth):
        j = l // 2
        mod_p, mod_s = mod_rows(mod_attn[j]) if l % 2 == 0 else (nmod_p, nmod_s)
        g_pre_l = g_pre[l].reshape(1, d)
        g_post_l = g_post[l].reshape(1, d)
        if l % 2 == 0:
            w_in = w_in_attn[j].astype(BF16)
            qn = q_norm[j].reshape(1, HEAD_DIM)
            kn = k_norm[j].reshape(1, HEAD_DIM)
            yp2 = y_p.reshape(n_p * t_p, d)
            ys2 = y_s.reshape(n_s * t_s, d)
            pp, = _inproj(yp2, mod_p, g_pre_l, w_in, rows_per_mod=n_p * t_p)
            ps, = _inproj(ys2, mod_s, g_pre_l, w_in, rows_per_mod=t_s)
            has_lru = l + 1 < depth
            og_p, k_p, v_p, w_out = _attention(pp, n_p, t_p, qn, kn, casts=(w_out_attn[j],), n_seq=4)
            ck = cache_k[:, j].reshape(n_s * n_ctx * N_KV_HEADS, HEAD_DIM)
            cv = cache_v[:, j].reshape(n_s * n_ctx * N_KV_HEADS, HEAD_DIM)
            og_s, *later = _attention(ps, n_s, t_s, qn, kn, cache=(ck, cv), rope=_rope_tables(t_s),
                                      casts=(w_in_lru[j], w_out_lru[j]) if has_lru else (),
                                      side_mod=(cond, w_mod, b_mod3, l + 1) if has_lru else None)
            nl = min(l + 1, depth - 1)
            if has_lru:
                lru_w = later[:2]
                nmod_p, nmod_s = mod_rows(later[2][0])
            else:
                nmod_p, nmod_s = mod_p, mod_s
            ng = g_pre[nl].reshape(1, d)
            y_p, h_p = _outproj(og_p, w_out, yp2, mod_p, g_post_l, nmod_p, ng, rows_per_mod=n_p * t_p)
            y_s, h_s = _outproj(og_s, w_out, ys2, mod_s, g_post_l, nmod_s, ng, rows_per_mod=t_s)
            y_p = y_p.reshape(n_p, t_p, d)
            y_s = y_s.reshape(n_s, t_s, d)
            new_k.append(k_p.reshape(n_p, t_p, N_KV_HEADS, HEAD_DIM))
            new_v.append(v_p.reshape(n_p, t_p, N_KV_HEADS, HEAD_DIM))
        else:
            w_in, w_out = lru_w
            cb = conv_b[j].reshape(1, LRU_WIDTH)
            lru_args = (conv_w[j], cb, w_rg_a[j], b_rg_a[j], w_rg_x[j], b_rg_x[j], lru_lambda[j])
            h0_p = jnp.zeros((2, n_p, LRU_WIDTH), F32)
            h0_s = jnp.swapaxes(state_lru[:, j], 0, 1)
            h0_s = jnp.concatenate([h0_s, h0_s], axis=1)
            yg_p, fin = _lru(h_p, w_in, n_p, t_p, *lru_args, h0_p, with_final=True)
            yg_s = _lru(h_s, w_in, n_s, t_s, *lru_args, h0_s, with_final=False)
            y_p = _outproj_perm(yg_p, w_out, y_p, mod_p, g_post_l)
            y_s = _outproj_perm(yg_s, w_out, y_s, mod_s, g_post_l)
            new_h.append(jnp.swapaxes(fin, 0, 1))
    new_cache_k = jnp.stack(new_k, axis=1)
    new_cache_v = jnp.stack(new_v, axis=1)
    new_state_lru = jnp.stack(new_h, axis=1)
    return (y_p, y_s, new_cache_k, new_cache_v, new_state_lru)
```
